```python
import math, functools
import jax, jax.numpy as jnp
from jax import lax
import numpy as np

D_MODEL = 2048
BATCH = 4
SEQ = 8192
DEPTH = 1
DEC_BATCH = 32
DEC_SEQ = 64
PAST_LEN = 2048

CHUNK = 64
N_META = 16
Q_BLOCK = 128
HEAD_DIM = 128
A_HEADS = 4
A_VDIM = 2 * HEAD_DIM
B_HEADS = 8
ROPE_DIM = HEAD_DIM // 4
ROPE_THETA = 500000.0
D_FF = 5632
EPS = 1e-6
FORGET_BIAS = 2.0
NEG = -1e30
A_Q = A_HEADS * 2 * HEAD_DIM
A_V = A_HEADS * A_VDIM
B_QK = B_HEADS * HEAD_DIM
N_IN = 2 * A_Q + A_V + 3 * B_QK + B_HEADS
SPLITS = (A_Q, 2 * A_Q, 2 * A_Q + A_V, 2 * A_Q + A_V + B_QK,
          2 * A_Q + A_V + 2 * B_QK, 2 * A_Q + A_V + 3 * B_QK)

kernel_name = 'hybrid_diff_fox_macaron_stream'


def _rmsnorm(x, g):
    xf = x.astype(jnp.float32)
    y = xf * lax.rsqrt(jnp.mean(xf * xf, axis=-1, keepdims=True) + EPS)
    return (y * g.astype(jnp.float32)).astype(x.dtype)


def _swiglu(h, w1, w3, w2):
    return (jax.nn.silu(h @ w1) * (h @ w3)) @ w2


def _rope(x, pos):
    half = ROPE_DIM // 2
    inv = jnp.power(ROPE_THETA, -jnp.arange(half, dtype=jnp.float32) * 2.0 / ROPE_DIM)
    ang = pos.astype(jnp.float32)[:, None] * inv[None, :]
    shape = (1, ang.shape[0]) + (1,) * (x.ndim - 3) + (half,)
    cos = jnp.cos(ang).reshape(shape)
    sin = jnp.sin(ang).reshape(shape)
    xr = x[..., :ROPE_DIM].astype(jnp.float32)
    x1, x2 = xr[..., :half], xr[..., half:]
    rot = jnp.concatenate([x1 * cos - x2 * sin, x2 * cos + x1 * sin], axis=-1)
    return jnp.concatenate([rot.astype(x.dtype), x[..., ROPE_DIM:]], axis=-1)


def _chunk_id(r):
    return jnp.where(r < N_META, -1, (r - N_META) // CHUNK)


def _project(h, pos, w_in, b_f, g_qa, g_ka, g_qb, g_kb):
    bsz, t = h.shape[:2]
    qa, ka, va, qb, kb, vb, fb = jnp.split(h @ w_in, SPLITS, axis=-1)
    qa = _rope(_rmsnorm(qa.reshape(bsz, t, A_HEADS, 2, HEAD_DIM), g_qa), pos)
    ka = _rope(_rmsnorm(ka.reshape(bsz, t, A_HEADS, 2, HEAD_DIM), g_ka), pos)
    va = va.reshape(bsz, t, A_HEADS, A_VDIM)
    qb = _rmsnorm(qb.reshape(bsz, t, B_HEADS, HEAD_DIM), g_qb)
    kb = _rmsnorm(kb.reshape(bsz, t, B_HEADS, HEAD_DIM), g_kb)
    vb = vb.reshape(bsz, t, B_HEADS, HEAD_DIM)
    logf = jax.nn.log_sigmoid((fb + b_f).astype(jnp.float32))
    return qa, ka, va, qb, kb, vb, logf


def _diff_attn(qa, ka, va, mask, lam):
    s = jnp.einsum('bqhmd,bkhmd->bhmqk', qa, ka).astype(jnp.float32) * (HEAD_DIM ** -0.5)
    p = jax.nn.softmax(jnp.where(mask, s, NEG), axis=-1)
    p = p[:, :, 0] - lam * p[:, :, 1]
    return jnp.einsum('bhqk,bkhe->bqhe', p.astype(va.dtype), va)


def _fox_attn(qb, kb, vb, cq, ck, mask):
    s = jnp.einsum('bqhd,bkhd->bhqk', qb, kb).astype(jnp.float32) * (HEAD_DIM ** -0.5)
    s = s + (jnp.transpose(cq, (0, 2, 1))[..., :, None] - jnp.transpose(ck, (0, 2, 1))[..., None, :])
    p = jax.nn.softmax(jnp.where(mask, s, NEG), axis=-1)
    return jnp.einsum('bhqk,bkhe->bqhe', p.astype(vb.dtype), vb)


def _mix_prompt(h, lam, w_in, b_f, g_qa, g_ka, g_qb, g_kb):
    bsz, L = h.shape[:2]
    qa, ka, va, qb, kb, vb, logf = _project(h, jnp.arange(L), w_in, b_f, g_qa, g_ka, g_qb, g_kb)
    c = jnp.cumsum(logf, axis=1)
    n_blk = -(-L // Q_BLOCK)
    Lp = n_blk * Q_BLOCK

    def to_blocks(a):
        a = jnp.pad(a, [(0, 0), (0, Lp - L)] + [(0, 0)] * (a.ndim - 2))
        return jnp.moveaxis(a.reshape((bsz, n_blk, Q_BLOCK) + a.shape[2:]), 1, 0)

    krow = jnp.arange(L)
    kchunk = _chunk_id(krow)

    def block(args):
        qa_b, qb_b, cq_b, start = args
        qrow = start + jnp.arange(Q_BLOCK)
        mask_a = kchunk[None, :] <= _chunk_id(qrow)[:, None]
        mask_b = krow[None, :] <= qrow[:, None]
        return (_diff_attn(qa_b, ka, va, mask_a, lam),
                _fox_attn(qb_b, kb, vb, cq_b, c, mask_b))

    starts = jnp.arange(n_blk) * Q_BLOCK
    oa, ob = lax.map(block, (to_blocks(qa), to_blocks(qb), to_blocks(c), starts))

    def from_blocks(o):
        return jnp.moveaxis(o, 0, 1).reshape((bsz, Lp) + o.shape[3:])[:, :L]

    return from_blocks(oa), from_blocks(ob), (ka, va, kb, vb, logf)


def _mix_sample(h, ck_a, cv_a, ck_b, cv_b, clogf, lam, w_in, b_f, g_qa, g_ka, g_qb, g_kb):
    bsz, t = h.shape[:2]
    past = ck_a.shape[1]
    pos = past + jnp.arange(t)
    qa, ka, va, qb, kb, vb, logf = _project(h, pos, w_in, b_f, g_qa, g_ka, g_qb, g_kb)
    ka_all = jnp.concatenate([ck_a.astype(ka.dtype), ka], axis=1)
    va_all = jnp.concatenate([cv_a.astype(va.dtype), va], axis=1)
    kb_all = jnp.concatenate([ck_b.astype(kb.dtype), kb], axis=1)
    vb_all = jnp.concatenate([cv_b.astype(vb.dtype), vb], axis=1)
    c = jnp.cumsum(jnp.concatenate([clogf.astype(jnp.float32), logf], axis=1), axis=1)
    mask_a = jnp.ones((t, past + t), dtype=bool)
    mask_b = jnp.arange(past + t)[None, :] <= pos[:, None]
    oa = _diff_attn(qa, ka_all, va_all, mask_a, lam)
    ob = _fox_attn(qb, kb_all, vb_all, c[:, past:], c, mask_b)
    return oa, ob, (ka, va, kb, vb, logf)


def _merge(oa, ob, g_oa, g_ob, w_out, lam_init):
    bsz, t = oa.shape[:2]
    oa = _rmsnorm(oa, g_oa) * (1.0 - lam_init)
    ob = _rmsnorm(ob, g_ob)
    return jnp.concatenate([oa.reshape(bsz, t, A_V), ob.reshape(bsz, t, B_QK)], axis=-1) @ w_out


def _layer(x, mixer, lam_init, g_ffn1, f1w1, f1w3, f1w2, g_mix, g_oa, g_ob, w_out,
           g_ffn2, f2w1, f2w3, f2w2, g_final):
    x = x + 0.5 * _swiglu(_rmsnorm(x, g_ffn1), f1w1, f1w3, f1w2)
    oa, ob, new = mixer(_rmsnorm(x, g_mix))
    x = x + _merge(oa, ob, g_oa, g_ob, w_out, lam_init)
    x = x + 0.5 * _swiglu(_rmsnorm(x, g_ffn2), f2w1, f2w3, f2w2)
    return _rmsnorm(x, g_final), new


def setup_inputs(seed: int = 0) -> dict:
    key = jax.random.key(seed)
    ks = iter(jax.random.split(key, 48))

    def nrm(shape, scale):
        return scale * jax.random.normal(next(ks), shape, jnp.float32)

    def gain(shape):
        return 1.0 + nrm(shape, 0.05)

    d = D_MODEL
    return {
        'x_prompt': nrm((BATCH, SEQ, d), 1.0),
        'x_sample': nrm((DEC_BATCH, DEC_SEQ, d), 1.0),
        'cache_a_k': nrm((DEPTH, DEC_BATCH, PAST_LEN, A_HEADS, 2, HEAD_DIM), 1.0),
        'cache_a_v': nrm((DEPTH, DEC_BATCH, PAST_LEN, A_HEADS, A_VDIM), 1.0),
        'cache_b_k': nrm((DEPTH, DEC_BATCH, PAST_LEN, B_HEADS, HEAD_DIM), 1.0),
        'cache_b_v': nrm((DEPTH, DEC_BATCH, PAST_LEN, B_HEADS, HEAD_DIM), 1.0),
        'cache_b_logf': jax.nn.log_sigmoid(FORGET_BIAS + nrm((DEPTH, DEC_BATCH, PAST_LEN, B_HEADS), 1.0)),
        'meta_tokens': nrm((N_META, d), 1.0),
        'g_ffn1': gain((DEPTH, d)),
        'ffn1_w1': nrm((DEPTH, d, D_FF), d ** -0.5),
        'ffn1_w3': nrm((DEPTH, d, D_FF), d ** -0.5),
        'ffn1_w2': nrm((DEPTH, D_FF, d), D_FF ** -0.5),
        'g_mix': gain((DEPTH, d)),
        'w_in': nrm((DEPTH, d, N_IN), d ** -0.5),
        'b_f': FORGET_BIAS + nrm((DEPTH, B_HEADS), 0.1),
        'g_qa': gain((DEPTH, HEAD_DIM)),
        'g_ka': gain((DEPTH, HEAD_DIM)),
        'g_qb': gain((DEPTH, HEAD_DIM)),
        'g_kb': gain((DEPTH, HEAD_DIM)),
        'lambda_q1': nrm((DEPTH, HEAD_DIM), 0.1),
        'lambda_k1': nrm((DEPTH, HEAD_DIM), 0.1),
        'lambda_q2': nrm((DEPTH, HEAD_DIM), 0.1),
        'lambda_k2': nrm((DEPTH, HEAD_DIM), 0.1),
        'g_oa': gain((DEPTH, A_VDIM)),
        'g_ob': gain((DEPTH, HEAD_DIM)),
        'w_out': nrm((DEPTH, A_V + B_QK, d), (A_V + B_QK) ** -0.5),
        'g_ffn2': gain((DEPTH, d)),
        'ffn2_w1': nrm((DEPTH, d, D_FF), d ** -0.5),
        'ffn2_w3': nrm((DEPTH, d, D_FF), d ** -0.5),
        'ffn2_w2': nrm((DEPTH, D_FF, d), D_FF ** -0.5),
        'g_final': gain((DEPTH, d)),
    }


def reference(x_prompt, x_sample, cache_a_k, cache_a_v, cache_b_k, cache_b_v, cache_b_logf,
              meta_tokens, g_ffn1, ffn1_w1, ffn1_w3, ffn1_w2, g_mix, w_in, b_f,
              g_qa, g_ka, g_qb, g_kb, lambda_q1, lambda_k1, lambda_q2, lambda_k2,
              g_oa, g_ob, w_out, g_ffn2, ffn2_w1, ffn2_w3, ffn2_w2, g_final):
    bsz = x_prompt.shape[0]
    meta = jnp.broadcast_to(meta_tokens.astype(x_prompt.dtype)[None], (bsz, N_META, x_prompt.shape[2]))
    xp = jnp.concatenate([meta, x_prompt], axis=1)
    xs = x_sample
    new_p = [[] for _ in range(5)]
    new_s = [[] for _ in range(5)]
    for l in range(DEPTH):
        lam_init = 0.8 - 0.6 * math.exp(-0.3 * l)
        lam = (jnp.exp(jnp.sum(lambda_q1[l].astype(jnp.float32) * lambda_k1[l].astype(jnp.float32)))
               - jnp.exp(jnp.sum(lambda_q2[l].astype(jnp.float32) * lambda_k2[l].astype(jnp.float32)))
               + lam_init)
        proj_w = (w_in[l], b_f[l], g_qa[l], g_ka[l], g_qb[l], g_kb[l])
        rest = (lam_init, g_ffn1[l], ffn1_w1[l], ffn1_w3[l], ffn1_w2[l], g_mix[l], g_oa[l], g_ob[l],
                w_out[l], g_ffn2[l], ffn2_w1[l], ffn2_w3[l], ffn2_w2[l], g_final[l])
        mix_p = functools.partial(_mix_prompt, lam=lam, w_in=proj_w[0], b_f=proj_w[1], g_qa=proj_w[2],
                                  g_ka=proj_w[3], g_qb=proj_w[4], g_kb=proj_w[5])
        mix_s = functools.partial(_mix_sample, ck_a=cache_a_k[l], cv_a=cache_a_v[l], ck_b=cache_b_k[l],
                                  cv_b=cache_b_v[l], clogf=cache_b_logf[l], lam=lam, w_in=proj_w[0],
                                  b_f=proj_w[1], g_qa=proj_w[2], g_ka=proj_w[3], g_qb=proj_w[4],
                                  g_kb=proj_w[5])
        xp, st_p = _layer(xp, mix_p, *rest)
        xs, st_s = _layer(xs, mix_s, *rest)
        for i in range(5):
            new_p[i].append(st_p[i])
            new_s[i].append(st_s[i])
    y_prompt = xp[:, N_META:]
    y_sample = xs
    return (y_prompt, y_sample,
            jnp.stack(new_p[0]), jnp.stack(new_p[1]), jnp.stack(new_p[2]), jnp.stack(new_p[3]), jnp.stack(new_p[4]),
            jnp.stack(new_s[0]), jnp.stack(new_s[1]), jnp.stack(new_s[2]), jnp.stack(new_s[3]), jnp.stack(new_s[4]))
```

```python
import functools
import math

import jax
import jax.numpy as jnp
import numpy as np
from jax import lax
from jax.experimental import pallas as pl
from jax.experimental.pallas import tpu as pltpu

F32 = jnp.float32
BF16 = jnp.bfloat16

HEAD_DIM = 128
A_HEADS = 4
B_HEADS = 8
SEG = 1024
N_SEG = 6
GROUPS = 4
GROUP_W = 2 * HEAD_DIM
N_META = 16
CHUNK = 64
ROPE_DIM = HEAD_DIM // 4
ROPE_THETA = 500000.0
EPS = 1e-6
NEG = -1e30
LANES = 128
VMEM_LIMIT = 60 * 1024 * 1024
SMALL_ROW_MULT = 512


def _params(*semantics):
    return pltpu.CompilerParams(dimension_semantics=semantics, vmem_limit_bytes=VMEM_LIMIT)


def _pick_tile(n, target, mult):
    best = None
    for t in range(mult, min(n, target) + 1, mult):
        if n % t == 0:
            best = t
    assert best is not None, (n, target, mult)
    return best


def _rms(x, g):
    ms = jnp.mean(x * x, axis=-1, keepdims=True)
    return (x * lax.rsqrt(ms + EPS)) * g


def _dot(a, b):
    return jnp.dot(a, b, preferred_element_type=F32)


def _dot_nt(a, b):
    return lax.dot_general(a, b, (((1,), (1,)), ((), ())), preferred_element_type=F32)


def _ffn_body(x_ref, g_ref, w1_ref, w3_ref, w2_ref, gout_ref, *rest, mode):
    if mode == "emit_norm":
        out_ref, h_ref, xn_scr = rest
    else:
        out_ref, xn_scr = rest
    f = pl.program_id(1)

    @pl.when(f == 0)
    def _():
        xn_scr[...] = _rms(x_ref[...], g_ref[...]).astype(BF16)
        out_ref[...] = jnp.zeros_like(out_ref)

    xn = xn_scr[...]
    h1 = _dot(xn, w1_ref[...])
    h3 = _dot(xn, w3_ref[...])
    gate = (h1 * jax.nn.sigmoid(h1)) * h3
    out_ref[...] += _dot(gate.astype(BF16), w2_ref[...])

    @pl.when(f == pl.num_programs(1) - 1)
    def _():
        y = x_ref[...] + 0.5 * out_ref[...]
        if mode == "emit_norm":
            out_ref[...] = y
            h_ref[...] = _rms(y, gout_ref[...]).astype(BF16)
        else:
            out_ref[...] = _rms(y, gout_ref[...])


def _ffn(x, g_in, w1, w3, w2, g_out, *, mode):
    t, d = x.shape
    f = w1.shape[1]
    tm = _pick_tile(t, 512, 16)
    tf = _pick_tile(f, 512, LANES)
    row = pl.BlockSpec((tm, d), lambda i, j: (i, 0))
    vec = pl.BlockSpec((1, d), lambda i, j: (0, 0))
    out_shape = [jax.ShapeDtypeStruct((t, d), F32)]
    out_specs = [row]
    if mode == "emit_norm":
        out_shape.append(jax.ShapeDtypeStruct((t, d), BF16))
        out_specs.append(row)
    res = pl.pallas_call(
        functools.partial(_ffn_body, mode=mode),
        grid=(t // tm, f // tf),
        in_specs=[row, vec,
                  pl.BlockSpec((d, tf), lambda i, j: (0, j)),
                  pl.BlockSpec((d, tf), lambda i, j: (0, j)),
                  pl.BlockSpec((tf, d), lambda i, j: (j, 0)),
                  vec],
        out_specs=out_specs,
        out_shape=out_shape,
        scratch_shapes=[pltpu.VMEM((tm, d), BF16)],
        compiler_params=_params("parallel", "arbitrary"),
        name="ffn_" + mode,
    )(x, g_in.reshape(1, d), w1, w3, w2, g_out.reshape(1, d))
    return res if mode == "emit_norm" else res[0]


def _proj_body(h_ref, w_ref, wf_ref, bf_ref, gqa_ref, gka_ref, gqb_ref, gkb_ref, tc_ref, ta_ref, tb_ref,
               qa_ref, kaf_ref, kab_ref, vaf_ref, vab_ref, qb_ref, kbf_ref, kbb_ref, vbf_ref, vbb_ref,
               logf_ref):
    j = pl.program_id(1)
    h = h_ref[...]
    q_scale = HEAD_DIM ** -0.5

    def heads(acc, g_ref, rope):
        out = []
        for c in range(SEG // HEAD_DIM):
            y = acc[:, c * HEAD_DIM:(c + 1) * HEAD_DIM]
            if g_ref is not None:
                y = _rms(y, g_ref[...])
            if rope:
                y = (y * tc_ref[...] + pltpu.roll(y, HEAD_DIM - ROPE_DIM // 2, 1) * ta_ref[...]
                     + pltpu.roll(y, ROPE_DIM // 2, 1) * tb_ref[...])
            out.append(y)
        return out

    def store(ys, f32_ref, bf_ref_, scale):
        for c, y in enumerate(ys):
            sl = slice(c * HEAD_DIM, (c + 1) * HEAD_DIM)
            if f32_ref is not None:
                f32_ref[:, sl] = y
            bf_ref_[:, sl] = (y if scale is None else y * scale).astype(BF16)

    @pl.when(j == 0)
    def _():
        store(heads(_dot(h, w_ref[...]), gqa_ref, True), None, qa_ref, q_scale)
        z = _dot(h, wf_ref[...]) + bf_ref[...]
        logf = jnp.minimum(z, 0.0) - jnp.log1p(jnp.exp(-jnp.abs(z)))
        logf_ref[...] = logf[:, :B_HEADS]

    @pl.when(j == 1)
    def _():
        store(heads(_dot(h, w_ref[...]), gka_ref, True), kaf_ref, kab_ref, None)

    @pl.when(j == 2)
    def _():
        store(heads(_dot(h, w_ref[...]), None, False), vaf_ref, vab_ref, None)

    @pl.when(j == 3)
    def _():
        store(heads(_dot(h, w_ref[...]), gqb_ref, False), None, qb_ref, q_scale)

    @pl.when(j == 4)
    def _():
        store(heads(_dot(h, w_ref[...]), gkb_ref, False), kbf_ref, kbb_ref, None)

    @pl.when(j == 5)
    def _():
        store(heads(_dot(h, w_ref[...]), None, False), vbf_ref, vbb_ref, None)


def _rope_tables(pos):
    half = ROPE_DIM // 2
    inv = jnp.power(ROPE_THETA, -jnp.arange(half, dtype=F32) * 2.0 / ROPE_DIM)
    ang = pos.astype(F32)[:, None] * inv[None, :]
    cos, sin = jnp.cos(ang), jnp.sin(ang)
    n = pos.shape[0]
    zeros = jnp.zeros((n, HEAD_DIM - ROPE_DIM), F32)
    tab_c = jnp.concatenate([cos, cos, jnp.ones((n, HEAD_DIM - ROPE_DIM), F32)], axis=1)
    tab_a = jnp.concatenate([-sin, jnp.zeros((n, half), F32), zeros], axis=1)
    tab_b = jnp.concatenate([jnp.zeros((n, half), F32), sin, zeros], axis=1)
    return tab_c, tab_a, tab_b


def _project(h, w_main, w_f, b_f, g_qa, g_ka, g_qb, g_kb, pos, *, rows_per_seq):
    t, d = h.shape
    tm = _pick_tile(rows_per_seq, 512, 16)
    tiles_per_seq = rows_per_seq // tm
    tabs = _rope_tables(pos)
    row = lambda w: pl.BlockSpec((tm, w), lambda i, j: (i, 0))
    gain = pl.BlockSpec((1, HEAD_DIM), lambda i, j: (0, 0))
    tab = pl.BlockSpec((tm, HEAD_DIM), lambda i, j: (i % tiles_per_seq, 0))
    bf = jax.ShapeDtypeStruct((t, SEG), BF16)
    f32 = jax.ShapeDtypeStruct((t, SEG), F32)
    return pl.pallas_call(
        _proj_body,
        grid=(t // tm, N_SEG),
        in_specs=[row(d),
                  pl.BlockSpec((d, SEG), lambda i, j: (0, j)),
                  pl.BlockSpec((d, LANES), lambda i, j: (0, 0)),
                  pl.BlockSpec((1, LANES), lambda i, j: (0, 0)),
                  gain, gain, gain, gain, tab, tab, tab],
        out_specs=[row(SEG)] * 10 + [row(B_HEADS)],
        out_shape=[bf, f32, bf, f32, bf, bf, f32, bf, f32, bf, jax.ShapeDtypeStruct((t, B_HEADS), F32)],
        compiler_params=_params("parallel", "arbitrary"),
        name="in_proj",
    )(h, w_main, w_f, b_f, g_qa.reshape(1, -1), g_ka.reshape(1, -1), g_qb.reshape(1, -1),
      g_kb.reshape(1, -1), *tabs)


def _cumsum_body(x_ref, upper_ref, earlier_ref, o_ref):
    sb, nb, _ = x_ref.shape
    hi = lax.Precision.HIGHEST
    x = x_ref[...].reshape(sb * nb, LANES)
    within = jnp.dot(x, upper_ref[...], precision=hi, preferred_element_type=F32)
    totals = jnp.broadcast_to(within[:, LANES - 1:LANES], within.shape)
    carry = jnp.dot(earlier_ref[...], totals, precision=hi, preferred_element_type=F32)
    o_ref[...] = (within + carry).reshape(sb, nb, LANES)


def _cumsum_lanes(x):
    s, length = x.shape
    lp = -(-length // (8 * LANES)) * (8 * LANES)
    nb = lp // LANES
    sb = _pick_tile(s, max(1, 512 // nb), 1)
    r = sb * nb
    xp = jnp.pad(x, ((0, 0), (0, lp - length))).reshape(s, nb, LANES)
    idx = np.arange(LANES)
    upper = jnp.asarray(idx[:, None] <= idx[None, :], F32)
    rid = np.arange(r)
    earlier = jnp.asarray((rid[None, :] < rid[:, None]) & (rid[None, :] // nb == rid[:, None] // nb), F32)
    blk = pl.BlockSpec((sb, nb, LANES), lambda i: (i, 0, 0))
    out = pl.pallas_call(
        _cumsum_body,
        grid=(s // sb,),
        in_specs=[blk, pl.BlockSpec((LANES, LANES), lambda i: (0, 0)), pl.BlockSpec((r, r), lambda i: (0, 0))],
        out_specs=blk,
        out_shape=jax.ShapeDtypeStruct((s, nb, LANES), F32),
        compiler_params=_params("parallel"),
        name="cumsum",
    )(xp, upper, earlier)
    return out.reshape(s, lp)[:, :length]


def _lam(lamv_ref, lam_init):
    v = lamv_ref[...]
    s1 = jnp.sum(v[0:1] * v[1:2], axis=-1, keepdims=True)
    s2 = jnp.sum(v[2:3] * v[3:4], axis=-1, keepdims=True)
    return jnp.exp(s1) - jnp.exp(s2) + lam_init


def _softmax_update(s, v, m_ref, l_ref, acc_ref):
    m_prev = m_ref[...]
    m_new = jnp.maximum(m_prev, jnp.max(s, axis=-1, keepdims=True))
    alpha = jnp.exp(m_prev - m_new)
    p = jnp.exp(s - m_new)
    l_ref[...] = alpha * l_ref[...] + jnp.sum(p, axis=-1, keepdims=True)
    acc_ref[...] = alpha * acc_ref[...] + _dot(p.astype(BF16), v)
    m_ref[...] = m_new


def _softmax_init(s, v, m_ref, l_ref, acc_ref):
    m = jnp.max(s, axis=-1, keepdims=True)
    p = jnp.exp(s - m)
    m_ref[...] = m
    l_ref[...] = jnp.sum(p, axis=-1, keepdims=True)
    acc_ref[...] = _dot(p.astype(BF16), v)


def _attn_a_body(qi_tab, ki_tab, q_ref, k_ref, v_ref, km_ref, vm_ref, lamv_ref, g_ref, o_ref,
                 m_scr, l_scr, acc_scr, *, lam_init):
    step = pl.program_id(2)
    qi, ki = qi_tab[step], ki_tab[step]
    tq, tk = q_ref.shape[0], k_ref.shape[0]
    maps = [slice(m * HEAD_DIM, (m + 1) * HEAD_DIM) for m in range(2)]

    @pl.when(ki == 0)
    def _():
        for m, sl in enumerate(maps):
            _softmax_init(_dot_nt(q_ref[:, sl], km_ref[:, sl]), vm_ref[...],
                          m_scr.at[m], l_scr.at[m], acc_scr.at[m])

    def update(diagonal):
        for m, sl in enumerate(maps):
            s = _dot_nt(q_ref[:, sl], k_ref[:, sl])
            if diagonal:
                rows = lax.broadcasted_iota(jnp.int32, (tq, tk), 0) // CHUNK
                cols = lax.broadcasted_iota(jnp.int32, (tq, tk), 1) // CHUNK
                s = jnp.where(cols <= rows, s, NEG)
            _softmax_update(s, v_ref[...], m_scr.at[m], l_scr.at[m], acc_scr.at[m])

    @pl.when(ki < qi)
    def _():
        update(False)

    @pl.when(ki == qi)
    def _():
        update(True)
        o0 = acc_scr[0] / l_scr[0]
        o1 = acc_scr[1] / l_scr[1]
        d = o0 - _lam(lamv_ref, lam_init) * o1
        o_ref[...] = (_rms(d, g_ref[...]) * (1.0 - lam_init)).astype(BF16)


def _attn_b_body(qi_tab, ki_tab, q_ref, k_ref, v_ref, km_ref, vm_ref, nc_ref, ncm_ref, g_ref, o_ref,
                 m_scr, l_scr, acc_scr):
    step = pl.program_id(2)
    qi, ki = qi_tab[step], ki_tab[step]
    tq, tk = q_ref.shape[0], k_ref.shape[0]
    heads = [slice(hh * HEAD_DIM, (hh + 1) * HEAD_DIM) for hh in range(2)]

    @pl.when(ki == 0)
    def _():
        for hh, sl in enumerate(heads):
            s = _dot_nt(q_ref[:, sl], km_ref[:, sl]) + ncm_ref[hh:hh + 1, :]
            _softmax_init(s, vm_ref[:, sl], m_scr.at[hh], l_scr.at[hh], acc_scr.at[hh])

    def update(diagonal):
        for hh, sl in enumerate(heads):
            s = _dot_nt(q_ref[:, sl], k_ref[:, sl]) + nc_ref[hh:hh + 1, :]
            if diagonal:
                rows = lax.broadcasted_iota(jnp.int32, (tq, tk), 0)
                cols = lax.broadcasted_iota(jnp.int32, (tq, tk), 1)
                s = jnp.where(cols <= rows, s, NEG)
            _softmax_update(s, v_ref[:, sl], m_scr.at[hh], l_scr.at[hh], acc_scr.at[hh])

    @pl.when(ki < qi)
    def _():
        update(False)

    @pl.when(ki == qi)
    def _():
        update(True)
        for hh, sl in enumerate(heads):
            o_ref[:, sl] = _rms(acc_scr[hh] / l_scr[hh], g_ref[...]).astype(BF16)


def _attn_prompt(mode, q, k, v, km, vm, extra, g_o, *, batch, seq, lam_init=None):
    tq = _pick_tile(seq, 512, CHUNK)
    nq = seq // tq
    pairs = [(i, j) for i in range(nq) for j in range(i + 1)]
    qi_tab = jnp.asarray([p[0] for p in pairs], jnp.int32)
    ki_tab = jnp.asarray([p[1] for p in pairs], jnp.int32)
    q_spec = pl.BlockSpec((tq, GROUP_W), lambda b, h, s, qt, kt: (b * nq + qt[s], h))
    k_spec = pl.BlockSpec((tq, GROUP_W), lambda b, h, s, qt, kt: (b * nq + kt[s], h))
    meta_spec = pl.BlockSpec((N_META, GROUP_W), lambda b, h, s, qt, kt: (0, h))
    if mode == "a":
        lamv, = extra
        body = functools.partial(_attn_a_body, lam_init=lam_init)
        extra_specs = [pl.BlockSpec(lamv.shape, lambda b, h, s, qt, kt: (0, 0))]
        acc_w = GROUP_W
    else:
        negc, negc_meta = extra
        body = _attn_b_body
        extra_specs = [pl.BlockSpec((None, None, 2, tq), lambda b, h, s, qt, kt: (b, h, 0, kt[s])),
                       pl.BlockSpec((None, 2, N_META), lambda b, h, s, qt, kt: (h, 0, 0))]
        acc_w = HEAD_DIM
    g_spec = pl.BlockSpec((1, g_o.shape[-1]), lambda b, h, s, qt, kt: (0, 0))
    return pl.pallas_call(
        body,
        grid_spec=pltpu.PrefetchScalarGridSpec(
            num_scalar_prefetch=2,
            grid=(batch, GROUPS, len(pairs)),
            in_specs=[q_spec, k_spec, k_spec, meta_spec, meta_spec] + extra_specs + [g_spec],
            out_specs=q_spec,
            scratch_shapes=[pltpu.VMEM((2, tq, 1), F32), pltpu.VMEM((2, tq, 1), F32),
                            pltpu.VMEM((2, tq, acc_w), F32)],
        ),
        out_shape=jax.ShapeDtypeStruct((batch * seq, SEG), BF16),
        compiler_params=_params("parallel", "parallel", "arbitrary"),
        name="attn_prompt_" + mode,
    )(qi_tab, ki_tab, q, k, v, km, vm, *extra, g_o.reshape(1, -1))


def _attn_sample_body(qa_ref, kan_ref, van_ref, cka_ref, cva_ref, qb_ref, kbn_ref, vbn_ref, ckb_ref, cvb_ref,
                      nc_ref, lamv_ref, goa_ref, gob_ref, oa_ref, ob_ref, *, past, lam_init):
    t = qa_ref.shape[0]

    def attend(q, k_old, k_new, v_old, v_new, bias_old, bias_new, causal):
        s_old = _dot_nt(q, k_old)
        s_new = _dot_nt(q, k_new)
        if bias_old is not None:
            s_old = s_old + bias_old
            s_new = s_new + bias_new
        if causal:
            rows = lax.broadcasted_iota(jnp.int32, (t, t), 0)
            cols = lax.broadcasted_iota(jnp.int32, (t, t), 1)
            s_new = jnp.where(cols <= rows, s_new, NEG)
        m = jnp.maximum(jnp.max(s_old, axis=-1, keepdims=True), jnp.max(s_new, axis=-1, keepdims=True))
        p_old = jnp.exp(s_old - m)
        p_new = jnp.exp(s_new - m)
        l = jnp.sum(p_old, axis=-1, keepdims=True) + jnp.sum(p_new, axis=-1, keepdims=True)
        return (_dot(p_old.astype(BF16), v_old) + _dot(p_new.astype(BF16), v_new)) / l

    cka = cka_ref[...].astype(BF16)
    cva = cva_ref[...].astype(BF16)
    outs = []
    for m in range(2):
        sl = slice(m * HEAD_DIM, (m + 1) * HEAD_DIM)
        outs.append(attend(qa_ref[:, sl], cka[:, sl], kan_ref[:, sl], cva, van_ref[...], None, None, False))
    d = outs[0] - _lam(lamv_ref, lam_init) * outs[1]
    oa_ref[...] = (_rms(d, goa_ref[...]) * (1.0 - lam_init)).astype(BF16)

    ckb = ckb_ref[...].astype(BF16)
    cvb = cvb_ref[...].astype(BF16)
    for hh in range(2):
        sl = slice(hh * HEAD_DIM, (hh + 1) * HEAD_DIM)
        o = attend(qb_ref[:, sl], ckb[:, sl], kbn_ref[:, sl], cvb[:, sl], vbn_ref[:, sl],
                   nc_ref[hh:hh + 1, :past], nc_ref[hh:hh + 1, past:past + t], True)
        ob_ref[:, sl] = _rms(o, gob_ref[...]).astype(BF16)


def _attn_sample(qa, kan, van, cka, cva, qb, kbn, vbn, ckb, cvb, negc, lamv, g_oa, g_ob, *,
                 batch, t, past, lam_init):
    new = pl.BlockSpec((t, GROUP_W), lambda b, h: (b, h))
    old = pl.BlockSpec((past, GROUP_W), lambda b, h: (b, h))
    whole = lambda a: pl.BlockSpec(a.shape, lambda b, h: (0,) * a.ndim)
    goa, gob = g_oa.reshape(1, -1), g_ob.reshape(1, -1)
    out = jax.ShapeDtypeStruct((batch * t, SEG), BF16)
    return pl.pallas_call(
        functools.partial(_attn_sample_body, past=past, lam_init=lam_init),
        grid=(batch, GROUPS),
        in_specs=[new, new, new, old, old, new, new, new, old, old,
                  pl.BlockSpec((None, None, 2, negc.shape[-1]), lambda b, h: (b, h, 0, 0)),
                  whole(lamv), whole(goa), whole(gob)],
        out_specs=[new, new],
        out_shape=[out, out],
        compiler_params=_params("parallel", "parallel"),
        name="attn_sample",
    )(qa, kan, van, cka, cva, qb, kbn, vbn, ckb, cvb, negc, lamv, goa, gob)


def _merge_body(x_ref, oa_ref, ob_ref, wa_ref, wb_ref, o_ref):
    o_ref[...] = x_ref[...] + _dot(oa_ref[...], wa_ref[...]) + _dot(ob_ref[...], wb_ref[...])


def _merge(x, oa, ob, w_a, w_b):
    t, d = x.shape
    tm = _pick_tile(t, 512, 16)
    row = lambda w: pl.BlockSpec((tm, w), lambda i: (i, 0))
    wsp = pl.BlockSpec((SEG, d), lambda i: (0, 0))
    return pl.pallas_call(
        _merge_body,
        grid=(t // tm,),
        in_specs=[row(d), row(SEG), row(SEG), wsp, wsp],
        out_specs=row(d),
        out_shape=jax.ShapeDtypeStruct((t, d), F32),
        compiler_params=_params("parallel"),
        name="out_proj",
    )(x, oa, ob, w_a, w_b)


def kernel(x_prompt, x_sample, cache_a_k, cache_a_v, cache_b_k, cache_b_v, cache_b_logf, meta_tokens, g_ffn1, ffn1_w1, ffn1_w3, ffn1_w2, g_mix, w_in, b_f, g_qa, g_ka, g_qb, g_kb, lambda_q1, lambda_k1, lambda_q2, lambda_k2, g_oa, g_ob, w_out, g_ffn2, ffn2_w1, ffn2_w3, ffn2_w2, g_final):
    depth = w_in.shape[0]
    assert depth == 1, "meta rows skip attention, which is only valid for a single layer"
    bsz, seq, d = x_prompt.shape
    dbsz, dseq, _ = x_sample.shape
    past = cache_a_k.shape[2]
    n_small = dbsz * dseq
    lam_init = 0.8 - 0.6 * math.exp(-0.3 * 0)
    bf = lambda a: a.astype(BF16)

    w_main = bf(w_in[0, :, :N_SEG * SEG])
    w_f = bf(jnp.pad(w_in[0, :, N_SEG * SEG:], ((0, 0), (0, LANES - B_HEADS))))
    b_fp = jnp.pad(b_f[0], (0, LANES - B_HEADS)).reshape(1, LANES)
    lamv = jnp.stack([lambda_q1[0], lambda_k1[0], lambda_q2[0], lambda_k2[0]]).astype(F32)
    wo_a, wo_b = bf(w_out[0, :SEG]), bf(w_out[0, SEG:])
    f1 = (g_ffn1[0], bf(ffn1_w1[0]), bf(ffn1_w3[0]), bf(ffn1_w2[0]), g_mix[0])
    f2 = (g_ffn2[0], bf(ffn2_w1[0]), bf(ffn2_w3[0]), bf(ffn2_w2[0]), g_final[0])
    gains = (g_qa[0], g_ka[0], g_qb[0], g_kb[0])

    xp = x_prompt.reshape(bsz * seq, d)
    n_pad = -(n_small + N_META) % SMALL_ROW_MULT
    xs = jnp.concatenate([x_sample.reshape(n_small, d), meta_tokens.astype(x_sample.dtype),
                          jnp.zeros((n_pad, d), x_sample.dtype)], axis=0)
    pos_p = N_META + jnp.arange(seq)
    pos_s = jnp.concatenate([jnp.tile(past + jnp.arange(dseq), dbsz), jnp.arange(N_META),
                             jnp.zeros((n_pad,), jnp.int32)])

    x1p, hp = _ffn(xp, *f1, mode="emit_norm")
    x1s, hs = _ffn(xs, *f1, mode="emit_norm")
    (qa_p, kaf_p, kab_p, vaf_p, vab_p, qb_p, kbf_p, kbb_p, vbf_p, vbb_p, logf_p) = _project(
        hp, w_main, w_f, b_fp, *gains, pos_p, rows_per_seq=seq)
    (qa_s, kaf_s, kab_s, vaf_s, vab_s, qb_s, kbf_s, kbb_s, vbf_s, vbb_s, logf_s) = _project(
        hs, w_main, w_f, b_fp, *gains, pos_s, rows_per_seq=xs.shape[0])

    logf_meta = logf_s[n_small:n_small + N_META]
    c_real = _cumsum_lanes(logf_p.reshape(bsz, seq, B_HEADS).transpose(0, 2, 1).reshape(bsz * B_HEADS, seq))
    c_meta = _cumsum_lanes(logf_meta.T)
    seq_s = jnp.concatenate([cache_b_logf[0].astype(F32), logf_s[:n_small].reshape(dbsz, dseq, B_HEADS)], axis=1)
    c_small = _cumsum_lanes(seq_s.transpose(0, 2, 1).reshape(dbsz * B_HEADS, past + dseq))
    negc_p = (-c_real).reshape(bsz, GROUPS, 2, seq)
    negc_meta = (c_meta[:, N_META - 1:] - c_meta).reshape(GROUPS, 2, N_META)
    negc_s = (-c_small).reshape(dbsz, GROUPS, 2, past + dseq)

    meta = slice(n_small, n_small + N_META)
    oa_p = _attn_prompt("a", qa_p, kab_p, vab_p, kab_s[meta], vab_s[meta], (lamv,), g_oa[0],
                        batch=bsz, seq=seq, lam_init=lam_init)
    ob_p = _attn_prompt("b", qb_p, kbb_p, vbb_p, kbb_s[meta], vbb_s[meta], (negc_p, negc_meta), g_ob[0],
                        batch=bsz, seq=seq)
    oa_s, ob_s = _attn_sample(
        qa_s, kab_s, vab_s,
        cache_a_k[0].reshape(dbsz * past, SEG), cache_a_v[0].reshape(dbsz * past, SEG),
        qb_s, kbb_s, vbb_s,
        cache_b_k[0].reshape(dbsz * past, SEG), cache_b_v[0].reshape(dbsz * past, SEG),
        negc_s, lamv, g_oa[0], g_ob[0], batch=dbsz, t=dseq, past=past, lam_init=lam_init)

    y_p = _ffn(_merge(x1p, oa_p, ob_p, wo_a, wo_b), *f2, mode="final_norm")
    y_s = _ffn(_merge(x1s[:n_small], oa_s, ob_s, wo_a, wo_b), *f2, mode="final_norm")

    def state(real, small, tail):
        m = jnp.broadcast_to(small[meta][None], (bsz, N_META, small.shape[-1]))
        p = jnp.concatenate([m, real.reshape(bsz, seq, -1)], axis=1).reshape((1, bsz, N_META + seq) + tail)
        return p, small[:n_small].reshape((1, dbsz, dseq) + tail)

    ak_p, ak_s = state(kaf_p, kaf_s, (A_HEADS, 2, HEAD_DIM))
    av_p, av_s = state(vaf_p, vaf_s, (A_HEADS, 2 * HEAD_DIM))
    bk_p, bk_s = state(kbf_p, kbf_s, (B_HEADS, HEAD_DIM))
    bv_p, bv_s = state(vbf_p, vbf_s, (B_HEADS, HEAD_DIM))
    lf_p, lf_s = state(logf_p, logf_s, (B_HEADS,))
    return (y_p.reshape(bsz, seq, d), y_s.reshape(dbsz, dseq, d),
            ak_p, av_p, bk_p, bv_p, lf_p, ak_s, av_s, bk_s, bv_s, lf_s)
```

```python
import functools
import math

import jax
import jax.numpy as jnp
import numpy as np
from jax import lax
from jax.experimental import pallas as pl
from jax.experimental.pallas import tpu as pltpu

F32 = jnp.float32
BF16 = jnp.bfloat16

HEAD_DIM = 128
A_HEADS = 4
B_HEADS = 8
SEG = 1024
N_SEG = 6
GROUPS = 4
GROUP_W = 2 * HEAD_DIM
N_META = 16
CHUNK = 64
ROPE_DIM = HEAD_DIM // 4
ROPE_THETA = 500000.0
EPS = 1e-6
NEG = -1e30
LOG2E = math.log2(math.e)
LANES = 128
VMEM_LIMIT = 60 * 1024 * 1024
SMALL_ROW_MULT = 512


def _params(*semantics):
    return pltpu.CompilerParams(dimension_semantics=semantics, vmem_limit_bytes=VMEM_LIMIT)


def _pick_tile(n, target, mult):
    best = None
    for t in range(mult, min(n, target) + 1, mult):
        if n % t == 0:
            best = t
    assert best is not None, (n, target, mult)
    return best


def _rms(x, g):
    ms = jnp.mean(x * x, axis=-1, keepdims=True)
    return (x * lax.rsqrt(ms + EPS)) * g


def _dot(a, b):
    return jnp.dot(a, b, preferred_element_type=F32)


def _dot_nt(a, b):
    return lax.dot_general(a, b, (((1,), (1,)), ((), ())), preferred_element_type=F32)


def _ffn_body(x_ref, g_ref, w1_ref, w3_ref, w2_ref, gout_ref, *rest, mode):
    if mode == "emit_norm":
        out_ref, h_ref, xn_scr = rest
    else:
        out_ref, xn_scr = rest
    f = pl.program_id(1)

    @pl.when(f == 0)
    def _():
        xn_scr[...] = _rms(x_ref[...], g_ref[...]).astype(BF16)
        out_ref[...] = jnp.zeros_like(out_ref)

    xn = xn_scr[...]
    h1 = _dot(xn, w1_ref[...])
    h3 = _dot(xn, w3_ref[...])
    gate = (h1 * jax.nn.sigmoid(h1)) * h3
    out_ref[...] += _dot(gate.astype(BF16), w2_ref[...])

    @pl.when(f == pl.num_programs(1) - 1)
    def _():
        y = x_ref[...] + 0.5 * out_ref[...]
        if mode == "emit_norm":
            out_ref[...] = y
            h_ref[...] = _rms(y, gout_ref[...]).astype(BF16)
        else:
            out_ref[...] = _rms(y, gout_ref[...])


def _ffn(x, g_in, w1, w3, w2, g_out, *, mode):
    t, d = x.shape
    f = w1.shape[1]
    tm = _pick_tile(t, 512, 16)
    tf = _pick_tile(f, 512, LANES)
    row = pl.BlockSpec((tm, d), lambda i, j: (i, 0))
    vec = pl.BlockSpec((1, d), lambda i, j: (0, 0))
    out_shape = [jax.ShapeDtypeStruct((t, d), F32)]
    out_specs = [row]
    if mode == "emit_norm":
        out_shape.append(jax.ShapeDtypeStruct((t, d), BF16))
        out_specs.append(row)
    res = pl.pallas_call(
        functools.partial(_ffn_body, mode=mode),
        grid=(t // tm, f // tf),
        in_specs=[row, vec,
                  pl.BlockSpec((d, tf), lambda i, j: (0, j)),
                  pl.BlockSpec((d, tf), lambda i, j: (0, j)),
                  pl.BlockSpec((tf, d), lambda i, j: (j, 0)),
                  vec],
        out_specs=out_specs,
        out_shape=out_shape,
        scratch_shapes=[pltpu.VMEM((tm, d), BF16)],
        compiler_params=_params("parallel", "arbitrary"),
        name="ffn_" + mode,
    )(x, g_in.reshape(1, d), w1, w3, w2, g_out.reshape(1, d))
    return res if mode == "emit_norm" else res[0]


def _proj_body(h_ref, w_ref, wf_ref, bf_ref, gqa_ref, gka_ref, gqb_ref, gkb_ref, tc_ref, ta_ref, tb_ref,
               qa_ref, kaf_ref, kab_ref, vaf_ref, vab_ref, qb_ref, kbf_ref, kbb_ref, vbf_ref, vbb_ref,
               logf_ref):
    j = pl.program_id(1)
    h = h_ref[...]
    q_scale = HEAD_DIM ** -0.5 * LOG2E

    def heads(acc, g_ref, rope):
        out = []
        for c in range(SEG // HEAD_DIM):
            y = acc[:, c * HEAD_DIM:(c + 1) * HEAD_DIM]
            if g_ref is not None:
                y = _rms(y, g_ref[...])
            if rope:
                y = (y * tc_ref[...] + pltpu.roll(y, HEAD_DIM - ROPE_DIM // 2, 1) * ta_ref[...]
                     + pltpu.roll(y, ROPE_DIM // 2, 1) * tb_ref[...])
            out.append(y)
        return out

    def store(ys, f32_ref, bf_ref_, scale):
        for c, y in enumerate(ys):
            sl = slice(c * HEAD_DIM, (c + 1) * HEAD_DIM)
            if f32_ref is not None:
                f32_ref[:, sl] = y
            bf_ref_[:, sl] = (y if scale is None else y * scale).astype(BF16)

    @pl.when(j == 0)
    def _():
        store(heads(_dot(h, w_ref[...]), gqa_ref, True), None, qa_ref, q_scale)
        z = _dot(h, wf_ref[...]) + bf_ref[...]
        logf = jnp.minimum(z, 0.0) - jnp.log1p(jnp.exp(-jnp.abs(z)))
        logf_ref[...] = logf[:, :B_HEADS]

    @pl.when(j == 1)
    def _():
        store(heads(_dot(h, w_ref[...]), gka_ref, True), kaf_ref, kab_ref, None)

    @pl.when(j == 2)
    def _():
        store(heads(_dot(h, w_ref[...]), None, False), vaf_ref, vab_ref, None)

    @pl.when(j == 3)
    def _():
        store(heads(_dot(h, w_ref[...]), gqb_ref, False), None, qb_ref, q_scale)

    @pl.when(j == 4)
    def _():
        store(heads(_dot(h, w_ref[...]), gkb_ref, False), kbf_ref, kbb_ref, None)

    @pl.when(j == 5)
    def _():
        store(heads(_dot(h, w_ref[...]), None, False), vbf_ref, vbb_ref, None)


def _rope_tables(pos):
    half = ROPE_DIM // 2
    inv = jnp.power(ROPE_THETA, -jnp.arange(half, dtype=F32) * 2.0 / ROPE_DIM)
    ang = pos.astype(F32)[:, None] * inv[None, :]
    cos, sin = jnp.cos(ang), jnp.sin(ang)
    n = pos.shape[0]
    zeros = jnp.zeros((n, HEAD_DIM - ROPE_DIM), F32)
    tab_c = jnp.concatenate([cos, cos, jnp.ones((n, HEAD_DIM - ROPE_DIM), F32)], axis=1)
    tab_a = jnp.concatenate([-sin, jnp.zeros((n, half), F32), zeros], axis=1)
    tab_b = jnp.concatenate([jnp.zeros((n, half), F32), sin, zeros], axis=1)
    return tab_c, tab_a, tab_b


def _project(h, w_main, w_f, b_f, g_qa, g_ka, g_qb, g_kb, pos, *, rows_per_seq):
    t, d = h.shape
    tm = _pick_tile(rows_per_seq, 512, 16)
    tiles_per_seq = rows_per_seq // tm
    tabs = _rope_tables(pos)
    row = lambda w: pl.BlockSpec((tm, w), lambda i, j: (i, 0))
    gain = pl.BlockSpec((1, HEAD_DIM), lambda i, j: (0, 0))
    tab = pl.BlockSpec((tm, HEAD_DIM), lambda i, j: (i % tiles_per_seq, 0))
    bf = jax.ShapeDtypeStruct((t, SEG), BF16)
    f32 = jax.ShapeDtypeStruct((t, SEG), F32)
    return pl.pallas_call(
        _proj_body,
        grid=(t // tm, N_SEG),
        in_specs=[row(d),
                  pl.BlockSpec((d, SEG), lambda i, j: (0, j)),
                  pl.BlockSpec((d, LANES), lambda i, j: (0, 0)),
                  pl.BlockSpec((1, LANES), lambda i, j: (0, 0)),
                  gain, gain, gain, gain, tab, tab, tab],
        out_specs=[row(SEG)] * 10 + [row(B_HEADS)],
        out_shape=[bf, f32, bf, f32, bf, bf, f32, bf, f32, bf, jax.ShapeDtypeStruct((t, B_HEADS), F32)],
        compiler_params=_params("parallel", "arbitrary"),
        name="in_proj",
    )(h, w_main, w_f, b_f, g_qa.reshape(1, -1), g_ka.reshape(1, -1), g_qb.reshape(1, -1),
      g_kb.reshape(1, -1), *tabs)


def _cumsum_body(x_ref, upper_ref, earlier_ref, o_ref):
    sb, nb, _ = x_ref.shape
    hi = lax.Precision.HIGHEST
    x = x_ref[...].reshape(sb * nb, LANES)
    within = jnp.dot(x, upper_ref[...], precision=hi, preferred_element_type=F32)
    totals = jnp.broadcast_to(within[:, LANES - 1:LANES], within.shape)
    carry = jnp.dot(earlier_ref[...], totals, precision=hi, preferred_element_type=F32)
    o_ref[...] = (within + carry).reshape(sb, nb, LANES)


def _cumsum_lanes(x):
    s, length = x.shape
    lp = -(-length // (8 * LANES)) * (8 * LANES)
    nb = lp // LANES
    sb = _pick_tile(s, max(1, 512 // nb), 1)
    r = sb * nb
    xp = jnp.pad(x, ((0, 0), (0, lp - length))).reshape(s, nb, LANES)
    idx = np.arange(LANES)
    upper = jnp.asarray(idx[:, None] <= idx[None, :], F32)
    rid = np.arange(r)
    earlier = jnp.asarray((rid[None, :] < rid[:, None]) & (rid[None, :] // nb == rid[:, None] // nb), F32)
    blk = pl.BlockSpec((sb, nb, LANES), lambda i: (i, 0, 0))
    out = pl.pallas_call(
        _cumsum_body,
        grid=(s // sb,),
        in_specs=[blk, pl.BlockSpec((LANES, LANES), lambda i: (0, 0)), pl.BlockSpec((r, r), lambda i: (0, 0))],
        out_specs=blk,
        out_shape=jax.ShapeDtypeStruct((s, nb, LANES), F32),
        compiler_params=_params("parallel"),
        name="cumsum",
    )(xp, upper, earlier)
    return out.reshape(s, lp)[:, :length]


def _lam(lamv_ref, lam_init):
    v = lamv_ref[...]
    s1 = jnp.sum(v[0:1] * v[1:2], axis=-1, keepdims=True)
    s2 = jnp.sum(v[2:3] * v[3:4], axis=-1, keepdims=True)
    return jnp.exp(s1) - jnp.exp(s2) + lam_init


def _lanes(col):
    return jnp.broadcast_to(col, (col.shape[0], LANES))


def _wide(stat, width):
    return jnp.tile(stat, (1, width // LANES))


def _with_ones(v):
    return jnp.concatenate([v, jnp.ones((v.shape[0], LANES), v.dtype)], axis=1)


def _probs(s, m_ref):
    m_prev = m_ref[...]
    m_new = jnp.maximum(m_prev, jnp.max(s, axis=1)[:, None])
    m_ref[...] = m_new
    return jnp.exp2(s - _wide(m_new, s.shape[1])), jnp.exp2(m_prev - m_new)


def _probs_init(s, m_ref):
    m = jnp.max(s, axis=1)[:, None]
    m_ref[...] = _lanes(m)
    return jnp.exp2(s - m)


def _attn_tile(seq):
    return _pick_tile(seq, 512, CHUNK)


def _key_tile(ref, kt, tk, cols):
    return ref[pl.ds(pl.multiple_of(kt * tk, tk), tk), cols]


def _attn_a_body(q_ref, k_ref, v_ref, km_ref, vm_ref, lamv_ref, g_ref, o_ref, m_scr, l_scr, acc_scr,
                 *, lam_init):
    qi = pl.program_id(2)
    tq = q_ref.shape[0]
    maps = [slice(m * HEAD_DIM, (m + 1) * HEAD_DIM) for m in range(2)]

    ps = []
    for m, sl in enumerate(maps):
        p = _probs_init(_dot_nt(q_ref[:, sl], km_ref[:, sl]), m_scr.at[m])
        l_scr[m] = _lanes(jnp.sum(p, axis=1)[:, None])
        ps.append(p.astype(BF16))
    pv = _dot(jnp.concatenate(ps, axis=0), vm_ref[...])
    for m in range(2):
        acc_scr[m] = pv[m * tq:(m + 1) * tq]

    def update(kt, diagonal):
        ps, alphas = [], []
        for m, sl in enumerate(maps):
            s = _dot_nt(q_ref[:, sl], _key_tile(k_ref, kt, tq, sl))
            if diagonal:
                rows = lax.broadcasted_iota(jnp.int32, s.shape, 0) // CHUNK
                cols = lax.broadcasted_iota(jnp.int32, s.shape, 1) // CHUNK
                s = jnp.where(cols <= rows, s, NEG)
            p, alpha = _probs(s, m_scr.at[m])
            l_scr[m] = alpha * l_scr[m] + _lanes(jnp.sum(p, axis=1)[:, None])
            ps.append(p.astype(BF16))
            alphas.append(alpha)
        pv = _dot(jnp.concatenate(ps, axis=0), _key_tile(v_ref, kt, tq, slice(None)))
        for m in range(2):
            acc_scr[m] = _wide(alphas[m], GROUP_W) * acc_scr[m] + pv[m * tq:(m + 1) * tq]

    def below_diagonal(kt, carry):
        update(kt, False)
        return carry

    lax.fori_loop(0, qi, below_diagonal, 0)
    update(qi, True)
    o0 = acc_scr[0] / _wide(l_scr[0], GROUP_W)
    o1 = acc_scr[1] / _wide(l_scr[1], GROUP_W)
    d = o0 - _lam(lamv_ref, lam_init) * o1
    o_ref[...] = (_rms(d, g_ref[...]) * (1.0 - lam_init)).astype(BF16)


def _attn_b_body(q_ref, k_ref, v_ref, km_ref, vm_ref, nc_ref, ncm_ref, g_ref, o_ref, m_scr, acc_scr):
    qi = pl.program_id(2)
    tq = q_ref.shape[0]
    heads = [slice(hh * HEAD_DIM, (hh + 1) * HEAD_DIM) for hh in range(2)]

    for hh, sl in enumerate(heads):
        s = _dot_nt(q_ref[:, sl], km_ref[:, sl]) + ncm_ref[hh:hh + 1, :] * LOG2E
        p = _probs_init(s, m_scr.at[hh])
        acc_scr[hh] = _dot(p.astype(BF16), _with_ones(vm_ref[:, sl]))

    def update(kt, diagonal):
        for hh, sl in enumerate(heads):
            s = _dot_nt(q_ref[:, sl], _key_tile(k_ref, kt, tq, sl)) + nc_ref[hh, pl.ds(kt, 1), :] * LOG2E
            if diagonal:
                rows = lax.broadcasted_iota(jnp.int32, s.shape, 0)
                cols = lax.broadcasted_iota(jnp.int32, s.shape, 1)
                s = jnp.where(cols <= rows, s, NEG)
            p, alpha = _probs(s, m_scr.at[hh])
            pv = _dot(p.astype(BF16), _with_ones(_key_tile(v_ref, kt, tq, sl)))
            acc_scr[hh] = _wide(alpha, GROUP_W) * acc_scr[hh] + pv

    def below_diagonal(kt, carry):
        update(kt, False)
        return carry

    lax.fori_loop(0, qi, below_diagonal, 0)
    update(qi, True)
    for hh, sl in enumerate(heads):
        acc = acc_scr[hh]
        o_ref[:, sl] = _rms(acc[:, :HEAD_DIM] / acc[:, HEAD_DIM:], g_ref[...]).astype(BF16)


def _attn_prompt(mode, q, k, v, km, vm, extra, g_o, *, batch, seq, lam_init=None):
    tq = _attn_tile(seq)
    nq = seq // tq
    q_spec = pl.BlockSpec((tq, GROUP_W), lambda b, h, i: (b * nq + i, h))
    kv_spec = pl.BlockSpec((seq, GROUP_W), lambda b, h, i: (b, h))
    meta_spec = pl.BlockSpec((N_META, GROUP_W), lambda b, h, i: (0, h))
    stat = pltpu.VMEM((2, tq, LANES), F32)
    acc = pltpu.VMEM((2, tq, GROUP_W), F32)
    if mode == "a":
        lamv, = extra
        body = functools.partial(_attn_a_body, lam_init=lam_init)
        extra_specs = [pl.BlockSpec(lamv.shape, lambda b, h, i: (0, 0))]
        scratch = [stat, stat, acc]
    else:
        negc, negc_meta = extra
        body = _attn_b_body
        extra_specs = [pl.BlockSpec((None, None, 2, nq, tq), lambda b, h, i: (b, h, 0, 0, 0)),
                       pl.BlockSpec((None, 2, N_META), lambda b, h, i: (h, 0, 0))]
        scratch = [stat, acc]
    g_spec = pl.BlockSpec((1, g_o.shape[-1]), lambda b, h, i: (0, 0))
    return pl.pallas_call(
        body,
        grid=(batch, GROUPS, nq),
        in_specs=[q_spec, kv_spec, kv_spec, meta_spec, meta_spec] + extra_specs + [g_spec],
        out_specs=q_spec,
        out_shape=jax.ShapeDtypeStruct((batch * seq, SEG), BF16),
        scratch_shapes=scratch,
        compiler_params=_params("parallel", "parallel", "arbitrary"),
        name="attn_prompt_" + mode,
    )(q, k, v, km, vm, *extra, g_o.reshape(1, -1))


def _attn_sample_body(qa_ref, kan_ref, van_ref, cka_ref, cva_ref, qb_ref, kbn_ref, vbn_ref, ckb_ref, cvb_ref,
                      nc_ref, lamv_ref, goa_ref, gob_ref, oa_ref, ob_ref, *, past, lam_init):
    t = qa_ref.shape[0]

    def attend(q, k_old, k_new, v_old, v_new, bias_old, bias_new, causal):
        s_old = _dot_nt(q, k_old)
        s_new = _dot_nt(q, k_new)
        if bias_old is not None:
            s_old = s_old + bias_old * LOG2E
            s_new = s_new + bias_new * LOG2E
        if causal:
            rows = lax.broadcasted_iota(jnp.int32, (t, t), 0)
            cols = lax.broadcasted_iota(jnp.int32, (t, t), 1)
            s_new = jnp.where(cols <= rows, s_new, NEG)
        m = jnp.maximum(jnp.max(s_old, axis=-1, keepdims=True), jnp.max(s_new, axis=-1, keepdims=True))
        p_old = jnp.exp2(s_old - m)
        p_new = jnp.exp2(s_new - m)
        l = jnp.sum(p_old, axis=-1, keepdims=True) + jnp.sum(p_new, axis=-1, keepdims=True)
        return (_dot(p_old.astype(BF16), v_old) + _dot(p_new.astype(BF16), v_new)) / l

    cka = cka_ref[...].astype(BF16)
    cva = cva_ref[...].astype(BF16)
    outs = []
    for m in range(2):
        sl = slice(m * HEAD_DIM, (m + 1) * HEAD_DIM)
        outs.append(attend(qa_ref[:, sl], cka[:, sl], kan_ref[:, sl], cva, van_ref[...], None, None, False))
    d = outs[0] - _lam(lamv_ref, lam_init) * outs[1]
    oa_ref[...] = (_rms(d, goa_ref[...]) * (1.0 - lam_init)).astype(BF16)

    ckb = ckb_ref[...].astype(BF16)
    cvb = cvb_ref[...].astype(BF16)
    for hh in range(2):
        sl = slice(hh * HEAD_DIM, (hh + 1) * HEAD_DIM)
        o = attend(qb_ref[:, sl], ckb[:, sl], kbn_ref[:, sl], cvb[:, sl], vbn_ref[:, sl],
                   nc_ref[hh:hh + 1, :past], nc_ref[hh:hh + 1, past:past + t], True)
        ob_ref[:, sl] = _rms(o, gob_ref[...]).astype(BF16)


def _attn_sample(qa, kan, van, cka, cva, qb, kbn, vbn, ckb, cvb, negc, lamv, g_oa, g_ob, *,
                 batch, t, past, lam_init):
    new = pl.BlockSpec((t, GROUP_W), lambda b, h: (b, h))
    old = pl.BlockSpec((past, GROUP_W), lambda b, h: (b, h))
    whole = lambda a: pl.BlockSpec(a.shape, lambda b, h: (0,) * a.ndim)
    goa, gob = g_oa.reshape(1, -1), g_ob.reshape(1, -1)
    out = jax.ShapeDtypeStruct((batch * t, SEG), BF16)
    return pl.pallas_call(
        functools.partial(_attn_sample_body, past=past, lam_init=lam_init),
        grid=(batch, GROUPS),
        in_specs=[new, new, new, old, old, new, new, new, old, old,
                  pl.BlockSpec((None, None, 2, negc.shape[-1]), lambda b, h: (b, h, 0, 0)),
                  whole(lamv), whole(goa), whole(gob)],
        out_specs=[new, new],
        out_shape=[out, out],
        compiler_params=_params("parallel", "parallel"),
        name="attn_sample",
    )(qa, kan, van, cka, cva, qb, kbn, vbn, ckb, cvb, negc, lamv, goa, gob)


def _merge_body(x_ref, oa_ref, ob_ref, wa_ref, wb_ref, o_ref):
    o_ref[...] = x_ref[...] + _dot(oa_ref[...], wa_ref[...]) + _dot(ob_ref[...], wb_ref[...])


def _merge(x, oa, ob, w_a, w_b):
    t, d = x.shape
    tm = _pick_tile(t, 512, 16)
    row = lambda w: pl.BlockSpec((tm, w), lambda i: (i, 0))
    wsp = pl.BlockSpec((SEG, d), lambda i: (0, 0))
    return pl.pallas_call(
        _merge_body,
        grid=(t // tm,),
        in_specs=[row(d), row(SEG), row(SEG), wsp, wsp],
        out_specs=row(d),
        out_shape=jax.ShapeDtypeStruct((t, d), F32),
        compiler_params=_params("parallel"),
        name="out_proj",
    )(x, oa, ob, w_a, w_b)


def kernel(x_prompt, x_sample, cache_a_k, cache_a_v, cache_b_k, cache_b_v, cache_b_logf, meta_tokens, g_ffn1, ffn1_w1, ffn1_w3, ffn1_w2, g_mix, w_in, b_f, g_qa, g_ka, g_qb, g_kb, lambda_q1, lambda_k1, lambda_q2, lambda_k2, g_oa, g_ob, w_out, g_ffn2, ffn2_w1, ffn2_w3, ffn2_w2, g_final):
    depth = w_in.shape[0]
    assert depth == 1, "meta rows skip attention, which is only valid for a single layer"
    bsz, seq, d = x_prompt.shape
    dbsz, dseq, _ = x_sample.shape
    past = cache_a_k.shape[2]
    n_small = dbsz * dseq
    lam_init = 0.8 - 0.6 * math.exp(-0.3 * 0)
    bf = lambda a: a.astype(BF16)

    w_main = bf(w_in[0, :, :N_SEG * SEG])
    w_f = bf(jnp.pad(w_in[0, :, N_SEG * SEG:], ((0, 0), (0, LANES - B_HEADS))))
    b_fp = jnp.pad(b_f[0], (0, LANES - B_HEADS)).reshape(1, LANES)
    lamv = jnp.stack([lambda_q1[0], lambda_k1[0], lambda_q2[0], lambda_k2[0]]).astype(F32)
    wo_a, wo_b = bf(w_out[0, :SEG]), bf(w_out[0, SEG:])
    f1 = (g_ffn1[0], bf(ffn1_w1[0]), bf(ffn1_w3[0]), bf(ffn1_w2[0]), g_mix[0])
    f2 = (g_ffn2[0], bf(ffn2_w1[0]), bf(ffn2_w3[0]), bf(ffn2_w2[0]), g_final[0])
    gains = (g_qa[0], g_ka[0], g_qb[0], g_kb[0])

    xp = x_prompt.reshape(bsz * seq, d)
    n_pad = -(n_small + N_META) % SMALL_ROW_MULT
    xs = jnp.concatenate([x_sample.reshape(n_small, d), meta_tokens.astype(x_sample.dtype),
                          jnp.zeros((n_pad, d), x_sample.dtype)], axis=0)
    pos_p = N_META + jnp.arange(seq)
    pos_s = jnp.concatenate([jnp.tile(past + jnp.arange(dseq), dbsz), jnp.arange(N_META),
                             jnp.zeros((n_pad,), jnp.int32)])

    x1p, hp = _ffn(xp, *f1, mode="emit_norm")
    x1s, hs = _ffn(xs, *f1, mode="emit_norm")
    (qa_p, kaf_p, kab_p, vaf_p, vab_p, qb_p, kbf_p, kbb_p, vbf_p, vbb_p, logf_p) = _project(
        hp, w_main, w_f, b_fp, *gains, pos_p, rows_per_seq=seq)
    (qa_s, kaf_s, kab_s, vaf_s, vab_s, qb_s, kbf_s, kbb_s, vbf_s, vbb_s, logf_s) = _project(
        hs, w_main, w_f, b_fp, *gains, pos_s, rows_per_seq=xs.shape[0])

    logf_meta = logf_s[n_small:n_small + N_META]
    c_real = _cumsum_lanes(logf_p.reshape(bsz, seq, B_HEADS).transpose(0, 2, 1).reshape(bsz * B_HEADS, seq))
    c_meta = _cumsum_lanes(logf_meta.T)
    seq_s = jnp.concatenate([cache_b_logf[0].astype(F32), logf_s[:n_small].reshape(dbsz, dseq, B_HEADS)], axis=1)
    c_small = _cumsum_lanes(seq_s.transpose(0, 2, 1).reshape(dbsz * B_HEADS, past + dseq))
    tq = _attn_tile(seq)
    negc_p = (-c_real).reshape(bsz, GROUPS, 2, seq // tq, tq)
    negc_meta = (c_meta[:, N_META - 1:] - c_meta).reshape(GROUPS, 2, N_META)
    negc_s = (-c_small).reshape(dbsz, GROUPS, 2, past + dseq)

    meta = slice(n_small, n_small + N_META)
    oa_p = _attn_prompt("a", qa_p, kab_p, vab_p, kab_s[meta], vab_s[meta], (lamv,), g_oa[0],
                        batch=bsz, seq=seq, lam_init=lam_init)
    ob_p = _attn_prompt("b", qb_p, kbb_p, vbb_p, kbb_s[meta], vbb_s[meta], (negc_p, negc_meta), g_ob[0],
                        batch=bsz, seq=seq)
    oa_s, ob_s = _attn_sample(
        qa_s, kab_s, vab_s,
        cache_a_k[0].reshape(dbsz * past, SEG), cache_a_v[0].reshape(dbsz * past, SEG),
        qb_s, kbb_s, vbb_s,
        cache_b_k[0].reshape(dbsz * past, SEG), cache_b_v[0].reshape(dbsz * past, SEG),
        negc_s, lamv, g_oa[0], g_ob[0], batch=dbsz, t=dseq, past=past, lam_init=lam_init)

    y_p = _ffn(_merge(x1p, oa_p, ob_p, wo_a, wo_b), *f2, mode="final_norm")
    y_s = _ffn(_merge(x1s[:n_small], oa_s, ob_s, wo_a, wo_b), *f2, mode="final_norm")

    def state(real, small, tail):
        m = jnp.broadcast_to(small[meta][None], (bsz, N_META, small.shape[-1]))
        p = jnp.concatenate([m, real.reshape(bsz, seq, -1)], axis=1).reshape((1, bsz, N_META + seq) + tail)
        return p, small[:n_small].reshape((1, dbsz, dseq) + tail)

    ak_p, ak_s = state(kaf_p, kaf_s, (A_HEADS, 2, HEAD_DIM))
    av_p, av_s = state(vaf_p, vaf_s, (A_HEADS, 2 * HEAD_DIM))
    bk_p, bk_s = state(kbf_p, kbf_s, (B_HEADS, HEAD_DIM))
    bv_p, bv_s = state(vbf_p, vbf_s, (B_HEADS, HEAD_DIM))
    lf_p, lf_s = state(logf_p, logf_s, (B_HEADS,))
    return (y_p.reshape(bsz, seq, d), y_s.reshape(dbsz, dseq, d),
            ak_p, av_p, bk_p, bv_p, lf_p, ak_s, av_s, bk_s, bv_s, lf_s)
```

```python
import functools
import math

import jax
import jax.numpy as jnp
import numpy as np
from jax import lax
from jax.experimental import pallas as pl
from jax.experimental.pallas import tpu as pltpu

F32 = jnp.float32
BF16 = jnp.bfloat16

HEAD_DIM = 128
A_HEADS = 4
B_HEADS = 8
SEG = 1024
N_SEG = 6
GROUPS = 4
GROUP_W = 2 * HEAD_DIM
PROMPT_UNITS = 4
PROMPT_GROUP_W = PROMPT_UNITS * HEAD_DIM
PROMPT_GROUPS = SEG // PROMPT_GROUP_W
ROW_SLOTS = SEG // HEAD_DIM
VA_SLOTS = [(c % 2) * A_HEADS + c // 2 for c in range(ROW_SLOTS)]
N_META = 16
CHUNK = 64
ROPE_DIM = HEAD_DIM // 4
ROPE_THETA = 500000.0
EPS = 1e-6
NEG = -1e30
LOG2E = math.log2(math.e)
LANES = 128
VMEM_LIMIT = 60 * 1024 * 1024
SMALL_ROW_MULT = 512


def _params(*semantics):
    return pltpu.CompilerParams(dimension_semantics=semantics, vmem_limit_bytes=VMEM_LIMIT)


def _pick_tile(n, target, mult):
    best = None
    for t in range(mult, min(n, target) + 1, mult):
        if n % t == 0:
            best = t
    assert best is not None, (n, target, mult)
    return best


def _rms(x, g):
    ms = jnp.mean(x * x, axis=-1, keepdims=True)
    return (x * lax.rsqrt(ms + EPS)) * g


def _dot(a, b):
    return jnp.dot(a, b, preferred_element_type=F32)


def _dot_nt(a, b):
    return lax.dot_general(a, b, (((1,), (1,)), ((), ())), preferred_element_type=F32)


def _ffn_body(x_ref, g_ref, w1_ref, w3_ref, w2_ref, gout_ref, *rest, mode):
    if mode == "emit_norm":
        out_ref, h_ref, xn_scr = rest
    else:
        out_ref, xn_scr = rest
    f = pl.program_id(1)

    @pl.when(f == 0)
    def _():
        xn_scr[...] = _rms(x_ref[...], g_ref[...]).astype(BF16)
        out_ref[...] = jnp.zeros_like(out_ref)

    xn = xn_scr[...]
    h1 = _dot(xn, w1_ref[...])
    h3 = _dot(xn, w3_ref[...])
    gate = (h1 * jax.nn.sigmoid(h1)) * h3
    out_ref[...] += _dot(gate.astype(BF16), w2_ref[...])

    @pl.when(f == pl.num_programs(1) - 1)
    def _():
        y = x_ref[...] + 0.5 * out_ref[...]
        if mode == "emit_norm":
            out_ref[...] = y
            h_ref[...] = _rms(y, gout_ref[...]).astype(BF16)
        else:
            out_ref[...] = _rms(y, gout_ref[...])


def _ffn(x, g_in, w1, w3, w2, g_out, *, mode):
    t, d = x.shape
    f = w1.shape[1]
    tm = _pick_tile(t, 512, 16)
    tf = _pick_tile(f, 512, LANES)
    row = pl.BlockSpec((tm, d), lambda i, j: (i, 0))
    vec = pl.BlockSpec((1, d), lambda i, j: (0, 0))
    out_shape = [jax.ShapeDtypeStruct((t, d), F32)]
    out_specs = [row]
    if mode == "emit_norm":
        out_shape.append(jax.ShapeDtypeStruct((t, d), BF16))
        out_specs.append(row)
    res = pl.pallas_call(
        functools.partial(_ffn_body, mode=mode),
        grid=(t // tm, f // tf),
        in_specs=[row, vec,
                  pl.BlockSpec((d, tf), lambda i, j: (0, j)),
                  pl.BlockSpec((d, tf), lambda i, j: (0, j)),
                  pl.BlockSpec((tf, d), lambda i, j: (j, 0)),
                  vec],
        out_specs=out_specs,
        out_shape=out_shape,
        scratch_shapes=[pltpu.VMEM((tm, d), BF16)],
        compiler_params=_params("parallel", "arbitrary"),
        name="ffn_" + mode,
    )(x, g_in.reshape(1, d), w1, w3, w2, g_out.reshape(1, d))
    return res if mode == "emit_norm" else res[0]


def _proj_body(h_ref, w_ref, wf_ref, bf_ref, gqa_ref, gka_ref, gqb_ref, gkb_ref, tc_ref, ta_ref, tb_ref,
               qa_ref, kaf_ref, kab_ref, vaf_ref, vab_ref, qb_ref, kbf_ref, kbb_ref, vbf_ref, vbb_ref,
               logf_ref):
    j = pl.program_id(1)
    h = h_ref[...]
    q_scale = HEAD_DIM ** -0.5 * LOG2E

    def heads(acc, g_ref, rope):
        out = []
        for c in range(SEG // HEAD_DIM):
            y = acc[:, c * HEAD_DIM:(c + 1) * HEAD_DIM]
            if g_ref is not None:
                y = _rms(y, g_ref[...])
            if rope:
                y = (y * tc_ref[...] + pltpu.roll(y, HEAD_DIM - ROPE_DIM // 2, 1) * ta_ref[...]
                     + pltpu.roll(y, ROPE_DIM // 2, 1) * tb_ref[...])
            out.append(y)
        return out

    def store(ys, f32_ref, bf_ref_, scale, slots=range(ROW_SLOTS)):
        for c, y in enumerate(ys):
            sl = slice(c * HEAD_DIM, (c + 1) * HEAD_DIM)
            if f32_ref is not None:
                f32_ref[pl.ds(slots[c], y.shape[0], stride=ROW_SLOTS), :] = y
            bf_ref_[:, sl] = (y if scale is None else y * scale).astype(BF16)

    @pl.when(j == 0)
    def _():
        store(heads(_dot(h, w_ref[...]), gqa_ref, True), None, qa_ref, q_scale)
        z = _dot(h, wf_ref[...]) + bf_ref[...]
        logf = jnp.minimum(z, 0.0) - jnp.log1p(jnp.exp(-jnp.abs(z)))
        logf_ref[...] = logf[:, :B_HEADS]

    @pl.when(j == 1)
    def _():
        store(heads(_dot(h, w_ref[...]), gka_ref, True), kaf_ref, kab_ref, None)

    @pl.when(j == 2)
    def _():
        store(heads(_dot(h, w_ref[...]), None, False), vaf_ref, vab_ref, None, VA_SLOTS)

    @pl.when(j == 3)
    def _():
        store(heads(_dot(h, w_ref[...]), gqb_ref, False), None, qb_ref, q_scale)

    @pl.when(j == 4)
    def _():
        store(heads(_dot(h, w_ref[...]), gkb_ref, False), kbf_ref, kbb_ref, None)

    @pl.when(j == 5)
    def _():
        store(heads(_dot(h, w_ref[...]), None, False), vbf_ref, vbb_ref, None)


def _rope_tables(pos):
    half = ROPE_DIM // 2
    inv = jnp.power(ROPE_THETA, -jnp.arange(half, dtype=F32) * 2.0 / ROPE_DIM)
    ang = pos.astype(F32)[:, None] * inv[None, :]
    cos, sin = jnp.cos(ang), jnp.sin(ang)
    n = pos.shape[0]
    zeros = jnp.zeros((n, HEAD_DIM - ROPE_DIM), F32)
    tab_c = jnp.concatenate([cos, cos, jnp.ones((n, HEAD_DIM - ROPE_DIM), F32)], axis=1)
    tab_a = jnp.concatenate([-sin, jnp.zeros((n, half), F32), zeros], axis=1)
    tab_b = jnp.concatenate([jnp.zeros((n, half), F32), sin, zeros], axis=1)
    return tab_c, tab_a, tab_b


def _project(h, w_main, w_f, b_f, g_qa, g_ka, g_qb, g_kb, pos, *, rows_per_seq):
    t, d = h.shape
    tm = _pick_tile(rows_per_seq, 512, 16)
    tiles_per_seq = rows_per_seq // tm
    tabs = _rope_tables(pos)
    row = lambda w: pl.BlockSpec((tm, w), lambda i, j: (i, 0))
    wide = row(SEG)
    tall = pl.BlockSpec((tm * ROW_SLOTS, HEAD_DIM), lambda i, j: (i, 0))
    gain = pl.BlockSpec((1, HEAD_DIM), lambda i, j: (0, 0))
    tab = pl.BlockSpec((tm, HEAD_DIM), lambda i, j: (i % tiles_per_seq, 0))
    bf = jax.ShapeDtypeStruct((t, SEG), BF16)
    f32 = jax.ShapeDtypeStruct((t * ROW_SLOTS, HEAD_DIM), F32)
    return pl.pallas_call(
        _proj_body,
        grid=(t // tm, N_SEG),
        in_specs=[row(d),
                  pl.BlockSpec((d, SEG), lambda i, j: (0, j)),
                  pl.BlockSpec((d, LANES), lambda i, j: (0, 0)),
                  pl.BlockSpec((1, LANES), lambda i, j: (0, 0)),
                  gain, gain, gain, gain, tab, tab, tab],
        out_specs=[wide, tall, wide, tall, wide, wide, tall, wide, tall, wide, row(B_HEADS)],
        out_shape=[bf, f32, bf, f32, bf, bf, f32, bf, f32, bf, jax.ShapeDtypeStruct((t, B_HEADS), F32)],
        compiler_params=_params("parallel", "arbitrary"),
        name="in_proj",
    )(h, w_main, w_f, b_f, g_qa.reshape(1, -1), g_ka.reshape(1, -1), g_qb.reshape(1, -1),
      g_kb.reshape(1, -1), *tabs)


def _cumsum_body(x_ref, upper_ref, earlier_ref, o_ref):
    sb, nb, _ = x_ref.shape
    hi = lax.Precision.HIGHEST
    x = x_ref[...].reshape(sb * nb, LANES)
    within = jnp.dot(x, upper_ref[...], precision=hi, preferred_element_type=F32)
    totals = jnp.broadcast_to(within[:, LANES - 1:LANES], within.shape)
    carry = jnp.dot(earlier_ref[...], totals, precision=hi, preferred_element_type=F32)
    o_ref[...] = (within + carry).reshape(sb, nb, LANES)


def _cumsum_lanes(x):
    s, length = x.shape
    lp = -(-length // (8 * LANES)) * (8 * LANES)
    nb = lp // LANES
    sb = _pick_tile(s, max(1, 512 // nb), 1)
    r = sb * nb
    xp = jnp.pad(x, ((0, 0), (0, lp - length))).reshape(s, nb, LANES)
    idx = np.arange(LANES)
    upper = jnp.asarray(idx[:, None] <= idx[None, :], F32)
    rid = np.arange(r)
    earlier = jnp.asarray((rid[None, :] < rid[:, None]) & (rid[None, :] // nb == rid[:, None] // nb), F32)
    blk = pl.BlockSpec((sb, nb, LANES), lambda i: (i, 0, 0))
    out = pl.pallas_call(
        _cumsum_body,
        grid=(s // sb,),
        in_specs=[blk, pl.BlockSpec((LANES, LANES), lambda i: (0, 0)), pl.BlockSpec((r, r), lambda i: (0, 0))],
        out_specs=blk,
        out_shape=jax.ShapeDtypeStruct((s, nb, LANES), F32),
        compiler_params=_params("parallel"),
        name="cumsum",
    )(xp, upper, earlier)
    return out.reshape(s, lp)[:, :length]


def _lam(lamv_ref, lam_init):
    v = lamv_ref[...]
    s1 = jnp.sum(v[0:1] * v[1:2], axis=-1, keepdims=True)
    s2 = jnp.sum(v[2:3] * v[3:4], axis=-1, keepdims=True)
    return jnp.exp(s1) - jnp.exp(s2) + lam_init


def _lanes(col):
    return jnp.broadcast_to(col, (col.shape[0], LANES))


def _wide(stat, width):
    return jnp.tile(stat, (1, width // LANES))


def _with_ones(v):
    return jnp.concatenate([v, jnp.ones((v.shape[0], LANES), v.dtype)], axis=1)


def _advance_max(ss, m_ref):
    m_prev = m_ref[...]
    m_new = m_prev
    for s in ss:
        m_new = jnp.maximum(m_new, jnp.max(s, axis=1)[:, None])
    m_ref[...] = m_new
    return m_new, jnp.exp2(m_prev - m_new)


def _diag_mask(s, chunk):
    rows = lax.broadcasted_iota(jnp.int32, s.shape, 0) // chunk
    cols = lax.broadcasted_iota(jnp.int32, s.shape, 1) // chunk
    return jnp.where(cols <= rows, s, NEG)


def _probs_init(s, m_ref):
    m = jnp.max(s, axis=1)[:, None]
    m_ref[...] = _lanes(m)
    return jnp.exp2(s - m)


def _attn_tile(seq):
    return _pick_tile(seq, 512, CHUNK)


def _key_tile(ref, kt, tk, cols):
    return ref[pl.ds(pl.multiple_of(kt * tk, tk), tk), cols]


def _pipelined_tiles(qi, step):
    def below_diagonal(kt, carry):
        step(kt, False)
        return carry

    lax.fori_loop(0, qi, below_diagonal, 0)
    step(qi, True)


def _attn_a_body(q_ref, k_ref, v_ref, km_ref, vm_ref, lamv_ref, g_ref, o_ref, m_scr, l_scr, acc_scr, p_scr,
                 *, lam_init):
    qi = pl.program_id(2)
    tq = q_ref.shape[0]
    wide_v = 2 * HEAD_DIM
    cols = lambda u: slice(u * HEAD_DIM, (u + 1) * HEAD_DIM)
    vcols = lambda u: slice((u // 2) * wide_v, (u // 2 + 1) * wide_v)

    for u in range(PROMPT_UNITS):
        p = _probs_init(_dot_nt(q_ref[:, cols(u)], km_ref[:, cols(u)]), m_scr.at[u])
        l_scr[u] = _lanes(jnp.sum(p, axis=1)[:, None])
        acc_scr[u] = _dot(p.astype(BF16), vm_ref[:, vcols(u)])
        p_scr[u] = jnp.zeros(p_scr.shape[1:], BF16)

    def step(kt, diagonal):
        prev = jnp.maximum(kt - 1, 0)
        for u in range(PROMPT_UNITS):
            s = _dot_nt(q_ref[:, cols(u)], _key_tile(k_ref, kt, tq, cols(u)))
            if diagonal:
                s = _diag_mask(s, CHUNK)
            flushed = acc_scr[u] + _dot(p_scr[u], _key_tile(v_ref, prev, tq, vcols(u)))
            m_new, alpha = _advance_max([s], m_scr.at[u])
            p = jnp.exp2(s - _wide(m_new, tq))
            l_scr[u] = alpha * l_scr[u] + _lanes(jnp.sum(p, axis=1)[:, None])
            p_scr[u] = p.astype(BF16)
            acc_scr[u] = _wide(alpha, wide_v) * flushed

    _pipelined_tiles(qi, step)
    lam = _lam(lamv_ref, lam_init)
    for h in range(PROMPT_UNITS // 2):
        o0, o1 = [(acc_scr[u] + _dot(p_scr[u], _key_tile(v_ref, qi, tq, vcols(u)))) / _wide(l_scr[u], wide_v)
                  for u in (2 * h, 2 * h + 1)]
        o_ref[:, vcols(2 * h)] = (_rms(o0 - lam * o1, g_ref[...]) * (1.0 - lam_init)).astype(BF16)


def _attn_b_body(q_ref, k_ref, v_ref, km_ref, vm_ref, nc_ref, ncm_ref, g_ref, o_ref, m_scr, acc_scr, p_scr):
    qi = pl.program_id(2)
    tq = q_ref.shape[0]
    cols = lambda u: slice(u * HEAD_DIM, (u + 1) * HEAD_DIM)

    for u in range(PROMPT_UNITS):
        s = _dot_nt(q_ref[:, cols(u)], km_ref[:, cols(u)]) + ncm_ref[u:u + 1, :] * LOG2E
        p = _probs_init(s, m_scr.at[u])
        acc_scr[u] = _dot(p.astype(BF16), _with_ones(vm_ref[:, cols(u)]))
        p_scr[u] = jnp.zeros(p_scr.shape[1:], BF16)

    def step(kt, diagonal):
        prev = jnp.maximum(kt - 1, 0)
        for u in range(PROMPT_UNITS):
            s = _dot_nt(q_ref[:, cols(u)], _key_tile(k_ref, kt, tq, cols(u))) + nc_ref[u, pl.ds(kt, 1), :] * LOG2E
            if diagonal:
                s = _diag_mask(s, 1)
            flushed = acc_scr[u] + _dot(p_scr[u], _with_ones(_key_tile(v_ref, prev, tq, cols(u))))
            m_new, alpha = _advance_max([s], m_scr.at[u])
            p_scr[u] = jnp.exp2(s - _wide(m_new, tq)).astype(BF16)
            acc_scr[u] = _wide(alpha, 2 * HEAD_DIM) * flushed

    _pipelined_tiles(qi, step)
    for u in range(PROMPT_UNITS):
        acc = acc_scr[u] + _dot(p_scr[u], _with_ones(_key_tile(v_ref, qi, tq, cols(u))))
        o_ref[:, cols(u)] = _rms(acc[:, :HEAD_DIM] / acc[:, HEAD_DIM:], g_ref[...]).astype(BF16)


def _attn_prompt(mode, q, k, v, km, vm, extra, g_o, *, batch, seq, lam_init=None):
    tq = _attn_tile(seq)
    nq = seq // tq
    q_spec = pl.BlockSpec((tq, PROMPT_GROUP_W), lambda b, h, i: (b * nq + i, h))
    kv_spec = pl.BlockSpec((seq, PROMPT_GROUP_W), lambda b, h, i: (b, h))
    meta_spec = pl.BlockSpec((N_META, PROMPT_GROUP_W), lambda b, h, i: (0, h))
    stat = pltpu.VMEM((PROMPT_UNITS, tq, LANES), F32)
    acc = pltpu.VMEM((PROMPT_UNITS, tq, 2 * HEAD_DIM), F32)
    pend = pltpu.VMEM((PROMPT_UNITS, tq, tq), BF16)
    if mode == "a":
        lamv, = extra
        body = functools.partial(_attn_a_body, lam_init=lam_init)
        extra_specs = [pl.BlockSpec(lamv.shape, lambda b, h, i: (0, 0))]
        scratch = [stat, stat, acc, pend]
    else:
        negc, negc_meta = extra
        body = _attn_b_body
        extra_specs = [pl.BlockSpec((None, None, PROMPT_UNITS, nq, tq), lambda b, h, i: (b, h, 0, 0, 0)),
                       pl.BlockSpec((None, PROMPT_UNITS, N_META), lambda b, h, i: (h, 0, 0))]
        scratch = [stat, acc, pend]
    g_spec = pl.BlockSpec((1, g_o.shape[-1]), lambda b, h, i: (0, 0))
    return pl.pallas_call(
        body,
        grid=(batch, PROMPT_GROUPS, nq),
        in_specs=[q_spec, kv_spec, kv_spec, meta_spec, meta_spec] + extra_specs + [g_spec],
        out_specs=q_spec,
        out_shape=jax.ShapeDtypeStruct((batch * seq, SEG), BF16),
        scratch_shapes=scratch,
        compiler_params=_params("parallel", "parallel", "arbitrary"),
        name="attn_prompt_" + mode,
    )(q, k, v, km, vm, *extra, g_o.reshape(1, -1))


def _attn_sample_body(qa_ref, kan_ref, van_ref, cka_ref, cva_ref, qb_ref, kbn_ref, vbn_ref, ckb_ref, cvb_ref,
                      nc_ref, lamv_ref, goa_ref, gob_ref, oa_ref, ob_ref, *, past, lam_init):
    t = qa_ref.shape[0]

    def attend(q, k_old, k_new, v_old, v_new, bias_old, bias_new, causal):
        s_old = _dot_nt(q, k_old)
        s_new = _dot_nt(q, k_new)
        if bias_old is not None:
            s_old = s_old + bias_old * LOG2E
            s_new = s_new + bias_new * LOG2E
        if causal:
            rows = lax.broadcasted_iota(jnp.int32, (t, t), 0)
            cols = lax.broadcasted_iota(jnp.int32, (t, t), 1)
            s_new = jnp.where(cols <= rows, s_new, NEG)
        m = jnp.maximum(jnp.max(s_old, axis=-1, keepdims=True), jnp.max(s_new, axis=-1, keepdims=True))
        p_old = jnp.exp2(s_old - m)
        p_new = jnp.exp2(s_new - m)
        l = jnp.sum(p_old, axis=-1, keepdims=True) + jnp.sum(p_new, axis=-1, keepdims=True)
        return (_dot(p_old.astype(BF16), v_old) + _dot(p_new.astype(BF16), v_new)) / l

    cka = cka_ref[...].astype(BF16)
    cva = cva_ref[...].astype(BF16)
    outs = []
    for m in range(2):
        sl = slice(m * HEAD_DIM, (m + 1) * HEAD_DIM)
        outs.append(attend(qa_ref[:, sl], cka[:, sl], kan_ref[:, sl], cva, van_ref[...], None, None, False))
    d = outs[0] - _lam(lamv_ref, lam_init) * outs[1]
    oa_ref[...] = (_rms(d, goa_ref[...]) * (1.0 - lam_init)).astype(BF16)

    ckb = ckb_ref[...].astype(BF16)
    cvb = cvb_ref[...].astype(BF16)
    for hh in range(2):
        sl = slice(hh * HEAD_DIM, (hh + 1) * HEAD_DIM)
        o = attend(qb_ref[:, sl], ckb[:, sl], kbn_ref[:, sl], cvb[:, sl], vbn_ref[:, sl],
                   nc_ref[hh:hh + 1, :past], nc_ref[hh:hh + 1, past:past + t], True)
        ob_ref[:, sl] = _rms(o, gob_ref[...]).astype(BF16)


def _attn_sample(qa, kan, van, cka, cva, qb, kbn, vbn, ckb, cvb, negc, lamv, g_oa, g_ob, *,
                 batch, t, past, lam_init):
    new = pl.BlockSpec((t, GROUP_W), lambda b, h: (b, h))
    old = pl.BlockSpec((past, GROUP_W), lambda b, h: (b, h))
    whole = lambda a: pl.BlockSpec(a.shape, lambda b, h: (0,) * a.ndim)
    goa, gob = g_oa.reshape(1, -1), g_ob.reshape(1, -1)
    out = jax.ShapeDtypeStruct((batch * t, SEG), BF16)
    return pl.pallas_call(
        functools.partial(_attn_sample_body, past=past, lam_init=lam_init),
        grid=(batch, GROUPS),
        in_specs=[new, new, new, old, old, new, new, new, old, old,
                  pl.BlockSpec((None, None, 2, negc.shape[-1]), lambda b, h: (b, h, 0, 0)),
                  whole(lamv), whole(goa), whole(gob)],
        out_specs=[new, new],
        out_shape=[out, out],
        compiler_params=_params("parallel", "parallel"),
        name="attn_sample",
    )(qa, kan, van, cka, cva, qb, kbn, vbn, ckb, cvb, negc, lamv, goa, gob)


def _merge_body(x_ref, oa_ref, ob_ref, wa_ref, wb_ref, o_ref):
    o_ref[...] = x_ref[...] + _dot(oa_ref[...], wa_ref[...]) + _dot(ob_ref[...], wb_ref[...])


def _merge(x, oa, ob, w_a, w_b):
    t, d = x.shape
    tm = _pick_tile(t, 512, 16)
    row = lambda w: pl.BlockSpec((tm, w), lambda i: (i, 0))
    wsp = pl.BlockSpec((SEG, d), lambda i: (0, 0))
    return pl.pallas_call(
        _merge_body,
        grid=(t // tm,),
        in_specs=[row(d), row(SEG), row(SEG), wsp, wsp],
        out_specs=row(d),
        out_shape=jax.ShapeDtypeStruct((t, d), F32),
        compiler_params=_params("parallel"),
        name="out_proj",
    )(x, oa, ob, w_a, w_b)


def kernel(x_prompt, x_sample, cache_a_k, cache_a_v, cache_b_k, cache_b_v, cache_b_logf, meta_tokens, g_ffn1, ffn1_w1, ffn1_w3, ffn1_w2, g_mix, w_in, b_f, g_qa, g_ka, g_qb, g_kb, lambda_q1, lambda_k1, lambda_q2, lambda_k2, g_oa, g_ob, w_out, g_ffn2, ffn2_w1, ffn2_w3, ffn2_w2, g_final):
    depth = w_in.shape[0]
    assert depth == 1, "meta rows skip attention, which is only valid for a single layer"
    bsz, seq, d = x_prompt.shape
    dbsz, dseq, _ = x_sample.shape
    past = cache_a_k.shape[2]
    n_small = dbsz * dseq
    lam_init = 0.8 - 0.6 * math.exp(-0.3 * 0)
    bf = lambda a: a.astype(BF16)

    w_main = bf(w_in[0, :, :N_SEG * SEG])
    w_f = bf(jnp.pad(w_in[0, :, N_SEG * SEG:], ((0, 0), (0, LANES - B_HEADS))))
    b_fp = jnp.pad(b_f[0], (0, LANES - B_HEADS)).reshape(1, LANES)
    lamv = jnp.stack([lambda_q1[0], lambda_k1[0], lambda_q2[0], lambda_k2[0]]).astype(F32)
    wo_a, wo_b = bf(w_out[0, :SEG]), bf(w_out[0, SEG:])
    f1 = (g_ffn1[0], bf(ffn1_w1[0]), bf(ffn1_w3[0]), bf(ffn1_w2[0]), g_mix[0])
    f2 = (g_ffn2[0], bf(ffn2_w1[0]), bf(ffn2_w3[0]), bf(ffn2_w2[0]), g_final[0])
    gains = (g_qa[0], g_ka[0], g_qb[0], g_kb[0])

    xp = x_prompt.reshape(bsz * seq, d)
    n_pad = -(n_small + N_META) % SMALL_ROW_MULT
    xs = jnp.concatenate([x_sample.reshape(n_small, d), meta_tokens.astype(x_sample.dtype),
                          jnp.zeros((n_pad, d), x_sample.dtype)], axis=0)
    pos_p = N_META + jnp.arange(seq)
    pos_s = jnp.concatenate([jnp.tile(past + jnp.arange(dseq), dbsz), jnp.arange(N_META),
                             jnp.zeros((n_pad,), jnp.int32)])

    x1p, hp = _ffn(xp, *f1, mode="emit_norm")
    x1s, hs = _ffn(xs, *f1, mode="emit_norm")
    (qa_p, kaf_p, kab_p, vaf_p, vab_p, qb_p, kbf_p, kbb_p, vbf_p, vbb_p, logf_p) = _project(
        hp, w_main, w_f, b_fp, *gains, pos_p, rows_per_seq=seq)
    (qa_s, kaf_s, kab_s, vaf_s, vab_s, qb_s, kbf_s, kbb_s, vbf_s, vbb_s, logf_s) = _project(
        hs, w_main, w_f, b_fp, *gains, pos_s, rows_per_seq=xs.shape[0])

    logf_meta = logf_s[n_small:n_small + N_META]
    c_real = _cumsum_lanes(logf_p.reshape(bsz, seq, B_HEADS).transpose(0, 2, 1).reshape(bsz * B_HEADS, seq))
    c_meta = _cumsum_lanes(logf_meta.T)
    seq_s = jnp.concatenate([cache_b_logf[0].astype(F32), logf_s[:n_small].reshape(dbsz, dseq, B_HEADS)], axis=1)
    c_small = _cumsum_lanes(seq_s.transpose(0, 2, 1).reshape(dbsz * B_HEADS, past + dseq))
    tq = _attn_tile(seq)
    negc_p = (-c_real).reshape(bsz, PROMPT_GROUPS, PROMPT_UNITS, seq // tq, tq)
    negc_meta = (c_meta[:, N_META - 1:] - c_meta).reshape(PROMPT_GROUPS, PROMPT_UNITS, N_META)
    negc_s = (-c_small).reshape(dbsz, GROUPS, 2, past + dseq)

    meta = slice(n_small, n_small + N_META)
    oa_p = _attn_prompt("a", qa_p, kab_p, vab_p, kab_s[meta], vab_s[meta], (lamv,), g_oa[0],
                        batch=bsz, seq=seq, lam_init=lam_init)
    ob_p = _attn_prompt("b", qb_p, kbb_p, vbb_p, kbb_s[meta], vbb_s[meta], (negc_p, negc_meta), g_ob[0],
                        batch=bsz, seq=seq)
    oa_s, ob_s = _attn_sample(
        qa_s, kab_s, vab_s,
        cache_a_k[0].reshape(dbsz * past, SEG), cache_a_v[0].reshape(dbsz * past, SEG),
        qb_s, kbb_s, vbb_s,
        cache_b_k[0].reshape(dbsz * past, SEG), cache_b_v[0].reshape(dbsz * past, SEG),
        negc_s, lamv, g_oa[0], g_ob[0], batch=dbsz, t=dseq, past=past, lam_init=lam_init)

    y_p = _ffn(_merge(x1p, oa_p, ob_p, wo_a, wo_b), *f2, mode="final_norm")
    y_s = _ffn(_merge(x1s[:n_small], oa_s, ob_s, wo_a, wo_b), *f2, mode="final_norm")

    def unslot(x, lead, tail, slots):
        if slots == VA_SLOTS:
            x = x.reshape(lead + (2, A_HEADS, HEAD_DIM)).swapaxes(-3, -2)
        return x.reshape(lead + tail)

    def state(real, small, tail, slots=None):
        per_tok = real.shape[0] // (bsz * seq)
        m = small[n_small * per_tok:(n_small + N_META) * per_tok]
        m = jnp.broadcast_to(m[None], (bsz,) + m.shape)
        p = jnp.concatenate([m, real.reshape(bsz, seq * per_tok, -1)], axis=1)
        return (unslot(p, (1, bsz, N_META + seq), tail, slots),
                unslot(small[:n_small * per_tok], (1, dbsz, dseq), tail, slots))

    ak_p, ak_s = state(kaf_p, kaf_s, (A_HEADS, 2, HEAD_DIM))
    av_p, av_s = state(vaf_p, vaf_s, (A_HEADS, 2 * HEAD_DIM), VA_SLOTS)
    bk_p, bk_s = state(kbf_p, kbf_s, (B_HEADS, HEAD_DIM))
    bv_p, bv_s = state(vbf_p, vbf_s, (B_HEADS, HEAD_DIM))
    lf_p, lf_s = state(logf_p, logf_s, (B_HEADS,))
    return (y_p.reshape(bsz, seq, d), y_s.reshape(dbsz, dseq, d),
            ak_p, av_p, bk_p, bv_p, lf_p, ak_s, av_s, bk_s, bv_s, lf_s)
```

```python
import functools
import math

import jax
import jax.numpy as jnp
import numpy as np
from jax import lax
from jax.experimental import pallas as pl
from jax.experimental.pallas import tpu as pltpu

F32 = jnp.float32
BF16 = jnp.bfloat16

HEAD_DIM = 128
A_HEADS = 4
B_HEADS = 8
SEG = 1024
N_SEG = 6
PROMPT_UNITS = 4
PROMPT_GROUP_W = PROMPT_UNITS * HEAD_DIM
PROMPT_GROUPS = SEG // PROMPT_GROUP_W
ROW_SLOTS = SEG // HEAD_DIM
VA_SLOTS = [(c % 2) * A_HEADS + c // 2 for c in range(ROW_SLOTS)]
N_META = 16
CHUNK = 64
ROPE_DIM = HEAD_DIM // 4
ROPE_THETA = 500000.0
EPS = 1e-6
NEG = -1e30
LOG2E = math.log2(math.e)
LANES = 128
VMEM_LIMIT = 60 * 1024 * 1024
SAMPLE_BLOCK = 512
SMALL_ROW_MULT = 512


def _params(*semantics):
    return pltpu.CompilerParams(dimension_semantics=semantics, vmem_limit_bytes=VMEM_LIMIT)


def _pick_tile(n, target, mult):
    best = None
    for t in range(mult, min(n, target) + 1, mult):
        if n % t == 0:
            best = t
    assert best is not None, (n, target, mult)
    return best


def _rms(x, g):
    ms = jnp.mean(x * x, axis=-1, keepdims=True)
    return (x * lax.rsqrt(ms + EPS)) * g


def _dot(a, b):
    return jnp.dot(a, b, preferred_element_type=F32)


def _dot_nt(a, b):
    return lax.dot_general(a, b, (((1,), (1,)), ((), ())), preferred_element_type=F32)


def _ffn_body(x_ref, g_ref, w1_ref, w3_ref, w2_ref, gout_ref, *rest, mode):
    if mode == "emit_norm":
        out_ref, h_ref, xn_scr = rest
    else:
        out_ref, xn_scr = rest
    f = pl.program_id(1)

    @pl.when(f == 0)
    def _():
        xn_scr[...] = _rms(x_ref[...], g_ref[...]).astype(BF16)
        out_ref[...] = jnp.zeros_like(out_ref)

    xn = xn_scr[...]
    h1 = _dot(xn, w1_ref[...])
    h3 = _dot(xn, w3_ref[...])
    gate = (h1 * jax.nn.sigmoid(h1)) * h3
    out_ref[...] += _dot(gate.astype(BF16), w2_ref[...])

    @pl.when(f == pl.num_programs(1) - 1)
    def _():
        y = x_ref[...] + 0.5 * out_ref[...]
        if mode == "emit_norm":
            out_ref[...] = y
            h_ref[...] = _rms(y, gout_ref[...]).astype(BF16)
        else:
            out_ref[...] = _rms(y, gout_ref[...])


def _ffn(x, g_in, w1, w3, w2, g_out, *, mode):
    t, d = x.shape
    f = w1.shape[1]
    tm = _pick_tile(t, 512, 16)
    tf = _pick_tile(f, 512, LANES)
    row = pl.BlockSpec((tm, d), lambda i, j: (i, 0))
    vec = pl.BlockSpec((1, d), lambda i, j: (0, 0))
    out_shape = [jax.ShapeDtypeStruct((t, d), F32)]
    out_specs = [row]
    if mode == "emit_norm":
        out_shape.append(jax.ShapeDtypeStruct((t, d), BF16))
        out_specs.append(row)
    res = pl.pallas_call(
        functools.partial(_ffn_body, mode=mode),
        grid=(t // tm, f // tf),
        in_specs=[row, vec,
                  pl.BlockSpec((d, tf), lambda i, j: (0, j)),
                  pl.BlockSpec((d, tf), lambda i, j: (0, j)),
                  pl.BlockSpec((tf, d), lambda i, j: (j, 0)),
                  vec],
        out_specs=out_specs,
        out_shape=out_shape,
        scratch_shapes=[pltpu.VMEM((tm, d), BF16)],
        compiler_params=_params("parallel", "arbitrary"),
        name="ffn_" + mode,
    )(x, g_in.reshape(1, d), w1, w3, w2, g_out.reshape(1, d))
    return res if mode == "emit_norm" else res[0]


def _proj_body(h_ref, w_ref, wf_ref, bf_ref, gqa_ref, gka_ref, gqb_ref, gkb_ref, tc_ref, ta_ref, tb_ref,
               qa_ref, kaf_ref, kab_ref, vaf_ref, vab_ref, qb_ref, kbf_ref, kbb_ref, vbf_ref, vbb_ref,
               logf_ref, acc_scr):
    j = pl.program_id(1)
    q_scale = HEAD_DIM ** -0.5 * LOG2E
    natural = list(range(ROW_SLOTS))
    segments = [(gqa_ref, True, None, None, qa_ref, q_scale),
                (gka_ref, True, kaf_ref, natural, kab_ref, None),
                (None, False, vaf_ref, VA_SLOTS, vab_ref, None),
                (gqb_ref, False, None, None, qb_ref, q_scale),
                (gkb_ref, False, kbf_ref, natural, kbb_ref, None),
                (None, False, vbf_ref, natural, vbb_ref, None)]

    def finish(k):
        g_ref, rope, f32_ref, slots, bf_ref_, scale = segments[k]
        for c in range(ROW_SLOTS):
            sl = slice(c * HEAD_DIM, (c + 1) * HEAD_DIM)
            y = acc_scr[k % 2, :, sl]
            if g_ref is not None:
                y = _rms(y, g_ref[...])
            if rope:
                y = (y * tc_ref[...] + pltpu.roll(y, HEAD_DIM - ROPE_DIM // 2, 1) * ta_ref[...]
                     + pltpu.roll(y, ROPE_DIM // 2, 1) * tb_ref[...])
            if f32_ref is not None:
                f32_ref[pl.ds(slots[c], y.shape[0], stride=ROW_SLOTS), :] = y
            bf_ref_[:, sl] = (y if scale is None else y * scale).astype(BF16)

    for k in range(N_SEG + 1):
        @pl.when(j == k)
        def _(k=k):
            if k < N_SEG:
                acc_scr[k % 2] = _dot(h_ref[...], w_ref[...])
            if k == 0:
                z = _dot(h_ref[...], wf_ref[...]) + bf_ref[...]
                logf = jnp.minimum(z, 0.0) - jnp.log1p(jnp.exp(-jnp.abs(z)))
                logf_ref[...] = logf[:, :B_HEADS]
            else:
                finish(k - 1)


def _rope_tables(pos):
    half = ROPE_DIM // 2
    inv = jnp.power(ROPE_THETA, -jnp.arange(half, dtype=F32) * 2.0 / ROPE_DIM)
    ang = pos.astype(F32)[:, None] * inv[None, :]
    cos, sin = jnp.cos(ang), jnp.sin(ang)
    n = pos.shape[0]
    zeros = jnp.zeros((n, HEAD_DIM - ROPE_DIM), F32)
    tab_c = jnp.concatenate([cos, cos, jnp.ones((n, HEAD_DIM - ROPE_DIM), F32)], axis=1)
    tab_a = jnp.concatenate([-sin, jnp.zeros((n, half), F32), zeros], axis=1)
    tab_b = jnp.concatenate([jnp.zeros((n, half), F32), sin, zeros], axis=1)
    return tab_c, tab_a, tab_b


def _project(h, w_main, w_f, b_f, g_qa, g_ka, g_qb, g_kb, pos, *, rows_per_seq):
    t, d = h.shape
    tm = _pick_tile(rows_per_seq, 512, 16)
    tiles_per_seq = rows_per_seq // tm
    tabs = _rope_tables(pos)
    row = lambda w: pl.BlockSpec((tm, w), lambda i, j: (i, 0))
    wide = row(SEG)
    tall = pl.BlockSpec((tm * ROW_SLOTS, HEAD_DIM), lambda i, j: (i, 0))
    gain = pl.BlockSpec((1, HEAD_DIM), lambda i, j: (0, 0))
    tab = pl.BlockSpec((tm, HEAD_DIM), lambda i, j: (i % tiles_per_seq, 0))
    bf = jax.ShapeDtypeStruct((t, SEG), BF16)
    f32 = jax.ShapeDtypeStruct((t * ROW_SLOTS, HEAD_DIM), F32)
    return pl.pallas_call(
        _proj_body,
        grid=(t // tm, N_SEG + 1),
        in_specs=[row(d),
                  pl.BlockSpec((d, SEG), lambda i, j: (0, jnp.minimum(j, N_SEG - 1))),
                  pl.BlockSpec((d, LANES), lambda i, j: (0, 0)),
                  pl.BlockSpec((1, LANES), lambda i, j: (0, 0)),
                  gain, gain, gain, gain, tab, tab, tab],
        out_specs=[wide, tall, wide, tall, wide, wide, tall, wide, tall, wide, row(B_HEADS)],
        out_shape=[bf, f32, bf, f32, bf, bf, f32, bf, f32, bf, jax.ShapeDtypeStruct((t, B_HEADS), F32)],
        scratch_shapes=[pltpu.VMEM((2, tm, SEG), F32)],
        compiler_params=_params("parallel", "arbitrary"),
        name="in_proj",
    )(h, w_main, w_f, b_f, g_qa.reshape(1, -1), g_ka.reshape(1, -1), g_qb.reshape(1, -1),
      g_kb.reshape(1, -1), *tabs)


def _cumsum_body(x_ref, upper_ref, earlier_ref, o_ref):
    sb, nb, _ = x_ref.shape
    hi = lax.Precision.HIGHEST
    x = x_ref[...].reshape(sb * nb, LANES)
    within = jnp.dot(x, upper_ref[...], precision=hi, preferred_element_type=F32)
    totals = jnp.broadcast_to(within[:, LANES - 1:LANES], within.shape)
    carry = jnp.dot(earlier_ref[...], totals, precision=hi, preferred_element_type=F32)
    o_ref[...] = (within + carry).reshape(sb, nb, LANES)


def _cumsum_lanes(x):
    s, length = x.shape
    lp = -(-length // (8 * LANES)) * (8 * LANES)
    nb = lp // LANES
    sb = _pick_tile(s, max(1, 512 // nb), 1)
    r = sb * nb
    xp = jnp.pad(x, ((0, 0), (0, lp - length))).reshape(s, nb, LANES)
    idx = np.arange(LANES)
    upper = jnp.asarray(idx[:, None] <= idx[None, :], F32)
    rid = np.arange(r)
    earlier = jnp.asarray((rid[None, :] < rid[:, None]) & (rid[None, :] // nb == rid[:, None] // nb), F32)
    blk = pl.BlockSpec((sb, nb, LANES), lambda i: (i, 0, 0))
    out = pl.pallas_call(
        _cumsum_body,
        grid=(s // sb,),
        in_specs=[blk, pl.BlockSpec((LANES, LANES), lambda i: (0, 0)), pl.BlockSpec((r, r), lambda i: (0, 0))],
        out_specs=blk,
        out_shape=jax.ShapeDtypeStruct((s, nb, LANES), F32),
        compiler_params=_params("parallel"),
        name="cumsum",
    )(xp, upper, earlier)
    return out.reshape(s, lp)[:, :length]


def _lam(lamv_ref, lam_init):
    v = lamv_ref[...]
    s1 = jnp.sum(v[0:1] * v[1:2], axis=-1, keepdims=True)
    s2 = jnp.sum(v[2:3] * v[3:4], axis=-1, keepdims=True)
    return jnp.exp(s1) - jnp.exp(s2) + lam_init


def _lanes(col):
    return jnp.broadcast_to(col, (col.shape[0], LANES))


def _wide(stat, width):
    return jnp.tile(stat, (1, width // LANES))


def _with_ones(v):
    return jnp.concatenate([v, jnp.ones((v.shape[0], LANES), v.dtype)], axis=1)


def _advance_max(ss, m_ref):
    m_prev = m_ref[...]
    m_new = m_prev
    for s in ss:
        m_new = jnp.maximum(m_new, jnp.max(s, axis=1)[:, None])
    m_ref[...] = m_new
    return m_new, jnp.exp2(m_prev - m_new)


def _diag_mask(s, chunk):
    rows = lax.broadcasted_iota(jnp.int32, s.shape, 0) // chunk
    cols = lax.broadcasted_iota(jnp.int32, s.shape, 1) // chunk
    return jnp.where(cols <= rows, s, NEG)


def _probs_init(s, m_ref):
    m = jnp.max(s, axis=1)[:, None]
    m_ref[...] = _lanes(m)
    return jnp.exp2(s - m)


def _attn_tile(seq):
    return _pick_tile(seq, 512, CHUNK)


def _key_tile(ref, kt, tk, cols):
    return ref[pl.ds(pl.multiple_of(kt * tk, tk), tk), cols]


def _pipelined_tiles(qi, step):
    def below_diagonal(kt, carry):
        step(kt, False)
        return carry

    lax.fori_loop(0, qi, below_diagonal, 0)
    step(qi, True)


def _attn_a_body(q_ref, k_ref, v_ref, km_ref, vm_ref, lamv_ref, g_ref, o_ref, m_scr, l_scr, acc_scr, p_scr,
                 *, lam_init):
    qi = pl.program_id(2)
    tq = q_ref.shape[0]
    wide_v = 2 * HEAD_DIM
    cols = lambda u: slice(u * HEAD_DIM, (u + 1) * HEAD_DIM)
    vcols = lambda u: slice((u // 2) * wide_v, (u // 2 + 1) * wide_v)

    for u in range(PROMPT_UNITS):
        p = _probs_init(_dot_nt(q_ref[:, cols(u)], km_ref[:, cols(u)]), m_scr.at[u])
        l_scr[u] = _lanes(jnp.sum(p, axis=1)[:, None])
        acc_scr[u] = _dot(p.astype(BF16), vm_ref[:, vcols(u)])
        p_scr[u] = jnp.zeros(p_scr.shape[1:], BF16)

    def step(kt, diagonal):
        prev = jnp.maximum(kt - 1, 0)
        for u in range(PROMPT_UNITS):
            s = _dot_nt(q_ref[:, cols(u)], _key_tile(k_ref, kt, tq, cols(u)))
            if diagonal:
                s = _diag_mask(s, CHUNK)
            flushed = acc_scr[u] + _dot(p_scr[u], _key_tile(v_ref, prev, tq, vcols(u)))
            m_new, alpha = _advance_max([s], m_scr.at[u])
            p = jnp.exp2(s - _wide(m_new, tq))
            l_scr[u] = alpha * l_scr[u] + _lanes(jnp.sum(p, axis=1)[:, None])
            p_scr[u] = p.astype(BF16)
            acc_scr[u] = _wide(alpha, wide_v) * flushed

    _pipelined_tiles(qi, step)
    lam = _lam(lamv_ref, lam_init)
    for h in range(PROMPT_UNITS // 2):
        o0, o1 = [(acc_scr[u] + _dot(p_scr[u], _key_tile(v_ref, qi, tq, vcols(u)))) / _wide(l_scr[u], wide_v)
                  for u in (2 * h, 2 * h + 1)]
        o_ref[:, vcols(2 * h)] = (_rms(o0 - lam * o1, g_ref[...]) * (1.0 - lam_init)).astype(BF16)


def _attn_b_body(q_ref, k_ref, v_ref, km_ref, vm_ref, nc_ref, ncm_ref, g_ref, o_ref, m_scr, acc_scr, p_scr):
    qi = pl.program_id(2)
    tq = q_ref.shape[0]
    cols = lambda u: slice(u * HEAD_DIM, (u + 1) * HEAD_DIM)

    for u in range(PROMPT_UNITS):
        s = _dot_nt(q_ref[:, cols(u)], km_ref[:, cols(u)]) + ncm_ref[u:u + 1, :] * LOG2E
        p = _probs_init(s, m_scr.at[u])
        acc_scr[u] = _dot(p.astype(BF16), _with_ones(vm_ref[:, cols(u)]))
        p_scr[u] = jnp.zeros(p_scr.shape[1:], BF16)

    def step(kt, diagonal):
        prev = jnp.maximum(kt - 1, 0)
        for u in range(PROMPT_UNITS):
            s = _dot_nt(q_ref[:, cols(u)], _key_tile(k_ref, kt, tq, cols(u))) + nc_ref[u, pl.ds(kt, 1), :] * LOG2E
            if diagonal:
                s = _diag_mask(s, 1)
            flushed = acc_scr[u] + _dot(p_scr[u], _with_ones(_key_tile(v_ref, prev, tq, cols(u))))
            m_new, alpha = _advance_max([s], m_scr.at[u])
            p_scr[u] = jnp.exp2(s - _wide(m_new, tq)).astype(BF16)
            acc_scr[u] = _wide(alpha, 2 * HEAD_DIM) * flushed

    _pipelined_tiles(qi, step)
    for u in range(PROMPT_UNITS):
        acc = acc_scr[u] + _dot(p_scr[u], _with_ones(_key_tile(v_ref, qi, tq, cols(u))))
        o_ref[:, cols(u)] = _rms(acc[:, :HEAD_DIM] / acc[:, HEAD_DIM:], g_ref[...]).astype(BF16)


def _attn_prompt(mode, q, k, v, km, vm, extra, g_o, *, batch, seq, lam_init=None):
    tq = _attn_tile(seq)
    nq = seq // tq
    q_spec = pl.BlockSpec((tq, PROMPT_GROUP_W), lambda b, h, i: (b * nq + i, h))
    kv_spec = pl.BlockSpec((seq, PROMPT_GROUP_W), lambda b, h, i: (b, h))
    meta_spec = pl.BlockSpec((N_META, PROMPT_GROUP_W), lambda b, h, i: (0, h))
    stat = pltpu.VMEM((PROMPT_UNITS, tq, LANES), F32)
    acc = pltpu.VMEM((PROMPT_UNITS, tq, 2 * HEAD_DIM), F32)
    pend = pltpu.VMEM((PROMPT_UNITS, tq, tq), BF16)
    if mode == "a":
        lamv, = extra
        body = functools.partial(_attn_a_body, lam_init=lam_init)
        extra_specs = [pl.BlockSpec(lamv.shape, lambda b, h, i: (0, 0))]
        scratch = [stat, stat, acc, pend]
    else:
        negc, negc_meta = extra
        body = _attn_b_body
        extra_specs = [pl.BlockSpec((None, None, PROMPT_UNITS, nq, tq), lambda b, h, i: (b, h, 0, 0, 0)),
                       pl.BlockSpec((None, PROMPT_UNITS, N_META), lambda b, h, i: (h, 0, 0))]
        scratch = [stat, acc, pend]
    g_spec = pl.BlockSpec((1, g_o.shape[-1]), lambda b, h, i: (0, 0))
    return pl.pallas_call(
        body,
        grid=(batch, PROMPT_GROUPS, nq),
        in_specs=[q_spec, kv_spec, kv_spec, meta_spec, meta_spec] + extra_specs + [g_spec],
        out_specs=q_spec,
        out_shape=jax.ShapeDtypeStruct((batch * seq, SEG), BF16),
        scratch_shapes=scratch,
        compiler_params=_params("parallel", "parallel", "arbitrary"),
        name="attn_prompt_" + mode,
    )(q, k, v, km, vm, *extra, g_o.reshape(1, -1))


def _slot_rows(ref, slot, n):
    return ref[pl.ds(slot, n, stride=ROW_SLOTS), :]


def _attn_sample_body(qa_ref, kan_ref, van_ref, cka_ref, cva_ref, qb_ref, kbn_ref, vbn_ref, ckb_ref, cvb_ref,
                      nco_ref, ncn_ref, lamv_ref, goa_ref, gob_ref, oa_ref, ob_ref,
                      m_scr, l_scr, acca_scr, accb_scr, *, lam_init):
    j = pl.program_id(1)
    t = qa_ref.shape[0]
    tb = cka_ref.shape[0] // ROW_SLOTS
    wide_v = 2 * HEAD_DIM
    cols = lambda u: slice(u * HEAD_DIM, (u + 1) * HEAD_DIM)
    n_a, n_b = 2 * A_HEADS, B_HEADS

    @pl.when(j == 0)
    def _():
        m_scr[...] = jnp.full(m_scr.shape, NEG, F32)
        l_scr[...] = jnp.zeros_like(l_scr)
        acca_scr[...] = jnp.zeros_like(acca_scr)
        accb_scr[...] = jnp.zeros_like(accb_scr)

    def update_a(u, s, v):
        m_new, alpha = _advance_max([s], m_scr.at[u])
        p = jnp.exp2(s - m_new[:, :1])
        l_scr[u] = alpha * l_scr[u] + _lanes(jnp.sum(p, axis=1)[:, None])
        acca_scr[u] = _wide(alpha, wide_v) * acca_scr[u] + _dot(p.astype(BF16), v)

    def update_b(h, s, v):
        m_new, alpha = _advance_max([s], m_scr.at[n_a + h])
        p = jnp.exp2(s - m_new[:, :1])
        accb_scr[h] = _wide(alpha, wide_v) * accb_scr[h] + _dot(p.astype(BF16), _with_ones(v))

    for h in range(A_HEADS):
        v = jnp.concatenate([_slot_rows(cva_ref, half * A_HEADS + h, tb) for half in range(2)], axis=1).astype(BF16)
        for m in range(2):
            u = 2 * h + m
            update_a(u, _dot_nt(qa_ref[:, cols(u)], _slot_rows(cka_ref, u, tb).astype(BF16)), v)
    for h in range(n_b):
        s = _dot_nt(qb_ref[:, cols(h)], _slot_rows(ckb_ref, h, tb).astype(BF16)) + nco_ref[h, pl.ds(j, 1), :] * LOG2E
        update_b(h, s, _slot_rows(cvb_ref, h, tb).astype(BF16))

    @pl.when(j == pl.num_programs(1) - 1)
    def _():
        lam = _lam(lamv_ref, lam_init)
        for h in range(A_HEADS):
            for m in range(2):
                u = 2 * h + m
                update_a(u, _dot_nt(qa_ref[:, cols(u)], kan_ref[:, cols(u)]), van_ref[:, h * wide_v:(h + 1) * wide_v])
            o0, o1 = [acca_scr[u] / _wide(l_scr[u], wide_v) for u in (2 * h, 2 * h + 1)]
            oa_ref[:, h * wide_v:(h + 1) * wide_v] = (_rms(o0 - lam * o1, goa_ref[...]) * (1.0 - lam_init)).astype(BF16)
        for h in range(n_b):
            s = _dot_nt(qb_ref[:, cols(h)], kbn_ref[:, cols(h)]) + ncn_ref[h:h + 1, :] * LOG2E
            update_b(h, _diag_mask(s, 1), vbn_ref[:, cols(h)])
            acc = accb_scr[h]
            ob_ref[:, cols(h)] = _rms(acc[:, :HEAD_DIM] / acc[:, HEAD_DIM:], gob_ref[...]).astype(BF16)


def _attn_sample(qa, kan, van, cka, cva, qb, kbn, vbn, ckb, cvb, negc_old, negc_new, lamv, g_oa, g_ob, *,
                 batch, t, past, lam_init):
    nblk, tb = negc_old.shape[-2:]
    new = pl.BlockSpec((t, SEG), lambda b, j: (b, 0))
    old = pl.BlockSpec((tb * ROW_SLOTS, HEAD_DIM), lambda b, j: (b * nblk + j, 0))
    whole = lambda a: pl.BlockSpec(a.shape, lambda b, j: (0,) * a.ndim)
    goa, gob = g_oa.reshape(1, -1), g_ob.reshape(1, -1)
    out = jax.ShapeDtypeStruct((batch * t, SEG), BF16)
    n_a = 2 * A_HEADS
    return pl.pallas_call(
        functools.partial(_attn_sample_body, lam_init=lam_init),
        grid=(batch, nblk),
        in_specs=[new, new, new, old, old, new, new, new, old, old,
                  pl.BlockSpec((None, B_HEADS, nblk, tb), lambda b, j: (b, 0, 0, 0)),
                  pl.BlockSpec((None, B_HEADS, t), lambda b, j: (b, 0, 0)),
                  whole(lamv), whole(goa), whole(gob)],
        out_specs=[new, new],
        out_shape=[out, out],
        scratch_shapes=[pltpu.VMEM((n_a + B_HEADS, t, LANES), F32), pltpu.VMEM((n_a, t, LANES), F32),
                        pltpu.VMEM((n_a, t, 2 * HEAD_DIM), F32), pltpu.VMEM((B_HEADS, t, 2 * HEAD_DIM), F32)],
        compiler_params=_params("parallel", "arbitrary"),
        name="attn_sample",
    )(qa, kan, van, cka, cva, qb, kbn, vbn, ckb, cvb, negc_old, negc_new, lamv, goa, gob)


def _merge_body(x_ref, oa_ref, ob_ref, wa_ref, wb_ref, o_ref):
    o_ref[...] = x_ref[...] + _dot(oa_ref[...], wa_ref[...]) + _dot(ob_ref[...], wb_ref[...])


def _merge(x, oa, ob, w_a, w_b):
    t, d = x.shape
    tm = _pick_tile(t, 512, 16)
    row = lambda w: pl.BlockSpec((tm, w), lambda i: (i, 0))
    wsp = pl.BlockSpec((SEG, d), lambda i: (0, 0))
    return pl.pallas_call(
        _merge_body,
        grid=(t // tm,),
        in_specs=[row(d), row(SEG), row(SEG), wsp, wsp],
        out_specs=row(d),
        out_shape=jax.ShapeDtypeStruct((t, d), F32),
        compiler_params=_params("parallel"),
        name="out_proj",
    )(x, oa, ob, w_a, w_b)


def kernel(x_prompt, x_sample, cache_a_k, cache_a_v, cache_b_k, cache_b_v, cache_b_logf, meta_tokens, g_ffn1, ffn1_w1, ffn1_w3, ffn1_w2, g_mix, w_in, b_f, g_qa, g_ka, g_qb, g_kb, lambda_q1, lambda_k1, lambda_q2, lambda_k2, g_oa, g_ob, w_out, g_ffn2, ffn2_w1, ffn2_w3, ffn2_w2, g_final):
    depth = w_in.shape[0]
    assert depth == 1, "meta rows skip attention, which is only valid for a single layer"
    bsz, seq, d = x_prompt.shape
    dbsz, dseq, _ = x_sample.shape
    past = cache_a_k.shape[2]
    n_small = dbsz * dseq
    lam_init = 0.8 - 0.6 * math.exp(-0.3 * 0)
    bf = lambda a: a.astype(BF16)

    w_main = bf(w_in[0, :, :N_SEG * SEG])
    w_f = bf(jnp.pad(w_in[0, :, N_SEG * SEG:], ((0, 0), (0, LANES - B_HEADS))))
    b_fp = jnp.pad(b_f[0], (0, LANES - B_HEADS)).reshape(1, LANES)
    lamv = jnp.stack([lambda_q1[0], lambda_k1[0], lambda_q2[0], lambda_k2[0]]).astype(F32)
    wo_a, wo_b = bf(w_out[0, :SEG]), bf(w_out[0, SEG:])
    f1 = (g_ffn1[0], bf(ffn1_w1[0]), bf(ffn1_w3[0]), bf(ffn1_w2[0]), g_mix[0])
    f2 = (g_ffn2[0], bf(ffn2_w1[0]), bf(ffn2_w3[0]), bf(ffn2_w2[0]), g_final[0])
    gains = (g_qa[0], g_ka[0], g_qb[0], g_kb[0])

    xp = x_prompt.reshape(bsz * seq, d)
    n_pad = -(n_small + N_META) % SMALL_ROW_MULT
    xs = jnp.concatenate([x_sample.reshape(n_small, d), meta_tokens.astype(x_sample.dtype),
                          jnp.zeros((n_pad, d), x_sample.dtype)], axis=0)
    pos_p = N_META + jnp.arange(seq)
    pos_s = jnp.concatenate([jnp.tile(past + jnp.arange(dseq), dbsz), jnp.arange(N_META),
                             jnp.zeros((n_pad,), jnp.int32)])

    x1p, hp = _ffn(xp, *f1, mode="emit_norm")
    x1s, hs = _ffn(xs, *f1, mode="emit_norm")
    (qa_p, kaf_p, kab_p, vaf_p, vab_p, qb_p, kbf_p, kbb_p, vbf_p, vbb_p, logf_p) = _project(
        hp, w_main, w_f, b_fp, *gains, pos_p, rows_per_seq=seq)
    (qa_s, kaf_s, kab_s, vaf_s, vab_s, qb_s, kbf_s, kbb_s, vbf_s, vbb_s, logf_s) = _project(
        hs, w_main, w_f, b_fp, *gains, pos_s, rows_per_seq=xs.shape[0])

    logf_meta = logf_s[n_small:n_small + N_META]
    c_real = _cumsum_lanes(logf_p.reshape(bsz, seq, B_HEADS).transpose(0, 2, 1).reshape(bsz * B_HEADS, seq))
    c_meta = _cumsum_lanes(logf_meta.T)
    seq_s = jnp.concatenate([cache_b_logf[0].astype(F32), logf_s[:n_small].reshape(dbsz, dseq, B_HEADS)], axis=1)
    c_small = _cumsum_lanes(seq_s.transpose(0, 2, 1).reshape(dbsz * B_HEADS, past + dseq))
    tq = _attn_tile(seq)
    negc_p = (-c_real).reshape(bsz, PROMPT_GROUPS, PROMPT_UNITS, seq // tq, tq)
    negc_meta = (c_meta[:, N_META - 1:] - c_meta).reshape(PROMPT_GROUPS, PROMPT_UNITS, N_META)
    tb = _pick_tile(past, SAMPLE_BLOCK, LANES)
    negc_s_old = (-c_small[:, :past]).reshape(dbsz, B_HEADS, past // tb, tb)
    negc_s_new = (-c_small[:, past:]).reshape(dbsz, B_HEADS, dseq)

    meta = slice(n_small, n_small + N_META)
    oa_p = _attn_prompt("a", qa_p, kab_p, vab_p, kab_s[meta], vab_s[meta], (lamv,), g_oa[0],
                        batch=bsz, seq=seq, lam_init=lam_init)
    ob_p = _attn_prompt("b", qb_p, kbb_p, vbb_p, kbb_s[meta], vbb_s[meta], (negc_p, negc_meta), g_ob[0],
                        batch=bsz, seq=seq)
    rows = lambda c: c[0].reshape(-1, HEAD_DIM)
    cva = cache_a_v[0].reshape(dbsz, past, A_HEADS, 2, HEAD_DIM).swapaxes(2, 3).reshape(-1, HEAD_DIM)
    oa_s, ob_s = _attn_sample(
        qa_s, kab_s, vab_s, rows(cache_a_k), cva, qb_s, kbb_s, vbb_s, rows(cache_b_k), rows(cache_b_v),
        negc_s_old, negc_s_new, lamv, g_oa[0], g_ob[0], batch=dbsz, t=dseq, past=past, lam_init=lam_init)

    y_p = _ffn(_merge(x1p, oa_p, ob_p, wo_a, wo_b), *f2, mode="final_norm")
    y_s = _ffn(_merge(x1s[:n_small], oa_s, ob_s, wo_a, wo_b), *f2, mode="final_norm")

    def unslot(x, lead, tail, slots):
        if slots == VA_SLOTS:
            x = x.reshape(lead + (2, A_HEADS, HEAD_DIM)).swapaxes(-3, -2)
        return x.reshape(lead + tail)

    def state(real, small, tail, slots=None):
        per_tok = real.shape[0] // (bsz * seq)
        m = unslot(small[n_small * per_tok:(n_small + N_META) * per_tok], (1, 1, N_META), tail, slots)
        m = jnp.broadcast_to(m, (1, bsz) + m.shape[2:])
        p = jnp.concatenate([m, unslot(real, (1, bsz, seq), tail, slots)], axis=2)
        return p, unslot(small[:n_small * per_tok], (1, dbsz, dseq), tail, slots)

    ak_p, ak_s = state(kaf_p, kaf_s, (A_HEADS, 2, HEAD_DIM))
    av_p, av_s = state(vaf_p, vaf_s, (A_HEADS, 2 * HEAD_DIM), VA_SLOTS)
    bk_p, bk_s = state(kbf_p, kbf_s, (B_HEADS, HEAD_DIM))
    bv_p, bv_s = state(vbf_p, vbf_s, (B_HEADS, HEAD_DIM))
    lf_p, lf_s = state(logf_p, logf_s, (B_HEADS,))
    return (y_p.reshape(bsz, seq, d), y_s.reshape(dbsz, dseq, d),
            ak_p, av_p, bk_p, bv_p, lf_p, ak_s, av_s, bk_s, bv_s, lf_s)
```

```python
import functools
import math

import jax
import jax.numpy as jnp
import numpy as np
from jax import lax
from jax.experimental import pallas as pl
from jax.experimental.pallas import tpu as pltpu

F32 = jnp.float32
BF16 = jnp.bfloat16

HEAD_DIM = 128
A_HEADS = 4
B_HEADS = 8
SEG = 1024
N_SEG = 6
PROMPT_UNITS = 4
PROMPT_GROUP_W = PROMPT_UNITS * HEAD_DIM
PROMPT_GROUPS = SEG // PROMPT_GROUP_W
ROW_SLOTS = SEG // HEAD_DIM
VA_SLOTS = [(c % 2) * A_HEADS + c // 2 for c in range(ROW_SLOTS)]
N_META = 16
CHUNK = 64
ROPE_DIM = HEAD_DIM // 4
ROPE_THETA = 500000.0
EPS = 1e-6
NEG = -1e30
LOG2E = math.log2(math.e)
LANES = 128
VMEM_LIMIT = 60 * 1024 * 1024
SAMPLE_BLOCK = 512
SMALL_ROW_MULT = 512


def _params(*semantics):
    return pltpu.CompilerParams(dimension_semantics=semantics, vmem_limit_bytes=VMEM_LIMIT)


def _pick_tile(n, target, mult):
    best = None
    for t in range(mult, min(n, target) + 1, mult):
        if n % t == 0:
            best = t
    assert best is not None, (n, target, mult)
    return best


def _rms(x, g):
    ms = jnp.mean(x * x, axis=-1, keepdims=True)
    return (x * lax.rsqrt(ms + EPS)) * g


def _dot(a, b):
    return jnp.dot(a, b, preferred_element_type=F32)


def _dot_nt(a, b):
    return lax.dot_general(a, b, (((1,), (1,)), ((), ())), preferred_element_type=F32)


def _ffn_body(x_ref, g_ref, w1_ref, w3_ref, w2_ref, gout_ref, *rest, mode):
    if mode == "emit_norm":
        out_ref, h_ref, xn_scr = rest
    else:
        out_ref, xn_scr = rest
    f = pl.program_id(1)

    @pl.when(f == 0)
    def _():
        xn_scr[...] = _rms(x_ref[...], g_ref[...]).astype(BF16)
        out_ref[...] = jnp.zeros_like(out_ref)

    xn = xn_scr[...]
    h1 = _dot(xn, w1_ref[...])
    h3 = _dot(xn, w3_ref[...])
    gate = (h1 * jax.nn.sigmoid(h1)) * h3
    out_ref[...] += _dot(gate.astype(BF16), w2_ref[...])

    @pl.when(f == pl.num_programs(1) - 1)
    def _():
        y = x_ref[...] + 0.5 * out_ref[...]
        if mode == "emit_norm":
            out_ref[...] = y
            h_ref[...] = _rms(y, gout_ref[...]).astype(BF16)
        else:
            out_ref[...] = _rms(y, gout_ref[...])


def _ffn(x, g_in, w1, w3, w2, g_out, *, mode):
    t, d = x.shape
    f = w1.shape[1]
    tm = _pick_tile(t, 512, 16)
    tf = _pick_tile(f, 512, LANES)
    row = pl.BlockSpec((tm, d), lambda i, j: (i, 0))
    vec = pl.BlockSpec((1, d), lambda i, j: (0, 0))
    out_shape = [jax.ShapeDtypeStruct((t, d), F32)]
    out_specs = [row]
    if mode == "emit_norm":
        out_shape.append(jax.ShapeDtypeStruct((t, d), BF16))
        out_specs.append(row)
    res = pl.pallas_call(
        functools.partial(_ffn_body, mode=mode),
        grid=(t // tm, f // tf),
        in_specs=[row, vec,
                  pl.BlockSpec((d, tf), lambda i, j: (0, j)),
                  pl.BlockSpec((d, tf), lambda i, j: (0, j)),
                  pl.BlockSpec((tf, d), lambda i, j: (j, 0)),
                  vec],
        out_specs=out_specs,
        out_shape=out_shape,
        scratch_shapes=[pltpu.VMEM((tm, d), BF16)],
        compiler_params=_params("parallel", "arbitrary"),
        name="ffn_" + mode,
    )(x, g_in.reshape(1, d), w1, w3, w2, g_out.reshape(1, d))
    return res if mode == "emit_norm" else res[0]


def _proj_body(h_ref, w_ref, wf_ref, bf_ref, gqa_ref, gka_ref, gqb_ref, gkb_ref, tc_ref, ta_ref, tb_ref,
               qa_ref, kaf_ref, kab_ref, vaf_ref, vab_ref, qb_ref, kbf_ref, kbb_ref, vbf_ref, vbb_ref,
               logf_ref, acc_scr):
    j = pl.program_id(1)
    q_scale = HEAD_DIM ** -0.5 * LOG2E
    natural = list(range(ROW_SLOTS))
    segments = [(gqa_ref, True, None, None, qa_ref, q_scale),
                (gka_ref, True, kaf_ref, natural, kab_ref, None),
                (None, False, vaf_ref, VA_SLOTS, vab_ref, None),
                (gqb_ref, False, None, None, qb_ref, q_scale),
                (gkb_ref, False, kbf_ref, natural, kbb_ref, None),
                (None, False, vbf_ref, natural, vbb_ref, None)]

    def finish(k):
        g_ref, rope, f32_ref, slots, bf_ref_, scale = segments[k]
        for c in range(ROW_SLOTS):
            sl = slice(c * HEAD_DIM, (c + 1) * HEAD_DIM)
            y = acc_scr[k % 2, :, sl]
            if g_ref is not None:
                y = _rms(y, g_ref[...])
            if rope:
                y = (y * tc_ref[...] + pltpu.roll(y, HEAD_DIM - ROPE_DIM // 2, 1) * ta_ref[...]
                     + pltpu.roll(y, ROPE_DIM // 2, 1) * tb_ref[...])
            if f32_ref is not None:
                f32_ref[pl.ds(slots[c], y.shape[0], stride=ROW_SLOTS), :] = y
            bf_ref_[:, sl] = (y if scale is None else y * scale).astype(BF16)

    for k in range(N_SEG + 1):
        @pl.when(j == k)
        def _(k=k):
            if k < N_SEG:
                acc_scr[k % 2] = _dot(h_ref[...], w_ref[...])
            if k == 0:
                z = _dot(h_ref[...], wf_ref[...]) + bf_ref[...]
                logf = jnp.minimum(z, 0.0) - jnp.log1p(jnp.exp(-jnp.abs(z)))
                logf_ref[...] = logf[:, :B_HEADS]
            else:
                finish(k - 1)


def _rope_tables(pos):
    half = ROPE_DIM // 2
    inv = jnp.power(ROPE_THETA, -jnp.arange(half, dtype=F32) * 2.0 / ROPE_DIM)
    ang = pos.astype(F32)[:, None] * inv[None, :]
    cos, sin = jnp.cos(ang), jnp.sin(ang)
    n = pos.shape[0]
    zeros = jnp.zeros((n, HEAD_DIM - ROPE_DIM), F32)
    tab_c = jnp.concatenate([cos, cos, jnp.ones((n, HEAD_DIM - ROPE_DIM), F32)], axis=1)
    tab_a = jnp.concatenate([-sin, jnp.zeros((n, half), F32), zeros], axis=1)
    tab_b = jnp.concatenate([jnp.zeros((n, half), F32), sin, zeros], axis=1)
    return tab_c, tab_a, tab_b


def _project(h, w_main, w_f, b_f, g_qa, g_ka, g_qb, g_kb, pos, *, rows_per_seq, gap=0):
    t, d = h.shape
    tm = _pick_tile(rows_per_seq, 512, 16)
    tiles_per_seq = rows_per_seq // tm
    tabs = _rope_tables(pos)
    row = lambda w: pl.BlockSpec((tm, w), lambda i, j: (i, 0))
    wide = row(SEG)
    first_row = lambda i: ((i // tiles_per_seq) * (gap + rows_per_seq) + gap + (i % tiles_per_seq) * tm) * ROW_SLOTS
    tall = pl.BlockSpec((pl.Element(tm * ROW_SLOTS), pl.Element(HEAD_DIM)), lambda i, j: (first_row(i), 0))
    gain = pl.BlockSpec((1, HEAD_DIM), lambda i, j: (0, 0))
    tab = pl.BlockSpec((tm, HEAD_DIM), lambda i, j: (i % tiles_per_seq, 0))
    bf = jax.ShapeDtypeStruct((t, SEG), BF16)
    f32 = jax.ShapeDtypeStruct(((t // rows_per_seq) * (gap + rows_per_seq) * ROW_SLOTS, HEAD_DIM), F32)
    return pl.pallas_call(
        _proj_body,
        grid=(t // tm, N_SEG + 1),
        in_specs=[row(d),
                  pl.BlockSpec((d, SEG), lambda i, j: (0, jnp.minimum(j, N_SEG - 1))),
                  pl.BlockSpec((d, LANES), lambda i, j: (0, 0)),
                  pl.BlockSpec((1, LANES), lambda i, j: (0, 0)),
                  gain, gain, gain, gain, tab, tab, tab],
        out_specs=[wide, tall, wide, tall, wide, wide, tall, wide, tall, wide, row(B_HEADS)],
        out_shape=[bf, f32, bf, f32, bf, bf, f32, bf, f32, bf, jax.ShapeDtypeStruct((t, B_HEADS), F32)],
        scratch_shapes=[pltpu.VMEM((2, tm, SEG), F32)],
        compiler_params=_params("parallel", "arbitrary"),
        name="in_proj",
    )(h, w_main, w_f, b_f, g_qa.reshape(1, -1), g_ka.reshape(1, -1), g_qb.reshape(1, -1),
      g_kb.reshape(1, -1), *tabs)


def _fill_body(*refs):
    n = len(refs) // 3
    for src_ref, out_ref in zip(refs[n:2 * n], refs[2 * n:]):
        out_ref[...] = src_ref[...]


def _fill_gaps(dsts, srcs, *, n_seq):
    gap_rows = srcs[0].shape[0]
    blocks_per_seq = dsts[0].shape[0] // n_seq // gap_rows
    assert blocks_per_seq * gap_rows * n_seq == dsts[0].shape[0]
    n = len(dsts)
    return pl.pallas_call(
        _fill_body,
        grid=(n_seq,),
        in_specs=[pl.BlockSpec(memory_space=pl.ANY)] * n + [pl.BlockSpec(srcs[0].shape, lambda b: (0, 0))] * n,
        out_specs=[pl.BlockSpec((gap_rows, HEAD_DIM), lambda b: (b * blocks_per_seq, 0))] * n,
        out_shape=[jax.ShapeDtypeStruct(x.shape, x.dtype) for x in dsts],
        input_output_aliases={k: k for k in range(n)},
        compiler_params=_params("parallel"),
        name="fill_meta_rows",
    )(*dsts, *srcs)


def _cumsum_body(x_ref, upper_ref, earlier_ref, o_ref):
    sb, nb, _ = x_ref.shape
    hi = lax.Precision.HIGHEST
    x = x_ref[...].reshape(sb * nb, LANES)
    within = jnp.dot(x, upper_ref[...], precision=hi, preferred_element_type=F32)
    totals = jnp.broadcast_to(within[:, LANES - 1:LANES], within.shape)
    carry = jnp.dot(earlier_ref[...], totals, precision=hi, preferred_element_type=F32)
    o_ref[...] = (within + carry).reshape(sb, nb, LANES)


def _cumsum_lanes(x):
    s, length = x.shape
    lp = -(-length // (8 * LANES)) * (8 * LANES)
    nb = lp // LANES
    sb = _pick_tile(s, max(1, 512 // nb), 1)
    r = sb * nb
    xp = jnp.pad(x, ((0, 0), (0, lp - length))).reshape(s, nb, LANES)
    idx = np.arange(LANES)
    upper = jnp.asarray(idx[:, None] <= idx[None, :], F32)
    rid = np.arange(r)
    earlier = jnp.asarray((rid[None, :] < rid[:, None]) & (rid[None, :] // nb == rid[:, None] // nb), F32)
    blk = pl.BlockSpec((sb, nb, LANES), lambda i: (i, 0, 0))
    out = pl.pallas_call(
        _cumsum_body,
        grid=(s // sb,),
        in_specs=[blk, pl.BlockSpec((LANES, LANES), lambda i: (0, 0)), pl.BlockSpec((r, r), lambda i: (0, 0))],
        out_specs=blk,
        out_shape=jax.ShapeDtypeStruct((s, nb, LANES), F32),
        compiler_params=_params("parallel"),
        name="cumsum",
    )(xp, upper, earlier)
    return out.reshape(s, lp)[:, :length]


def _lam(lamv_ref, lam_init):
    v = lamv_ref[...]
    s1 = jnp.sum(v[0:1] * v[1:2], axis=-1, keepdims=True)
    s2 = jnp.sum(v[2:3] * v[3:4], axis=-1, keepdims=True)
    return jnp.exp(s1) - jnp.exp(s2) + lam_init


def _lanes(col):
    return jnp.broadcast_to(col, (col.shape[0], LANES))


def _wide(stat, width):
    return jnp.tile(stat, (1, width // LANES))


def _with_ones(v):
    return jnp.concatenate([v, jnp.ones((v.shape[0], LANES), v.dtype)], axis=1)


def _advance_max(ss, m_ref):
    m_prev = m_ref[...]
    m_new = m_prev
    for s in ss:
        m_new = jnp.maximum(m_new, jnp.max(s, axis=1)[:, None])
    m_ref[...] = m_new
    return m_new, jnp.exp2(m_prev - m_new)


def _diag_mask(s, chunk):
    rows = lax.broadcasted_iota(jnp.int32, s.shape, 0) // chunk
    cols = lax.broadcasted_iota(jnp.int32, s.shape, 1) // chunk
    return jnp.where(cols <= rows, s, NEG)


def _probs_init(s, m_ref):
    m = jnp.max(s, axis=1)[:, None]
    m_ref[...] = _lanes(m)
    return jnp.exp2(s - m)


def _attn_tile(seq):
    return _pick_tile(seq, 512, CHUNK)


def _key_tile(ref, kt, tk, cols):
    return ref[pl.ds(pl.multiple_of(kt * tk, tk), tk), cols]


def _pipelined_tiles(qi, step):
    def below_diagonal(i, carry):
        step(2 * i, False)
        step(2 * i + 1, False)
        return carry

    lax.fori_loop(0, qi // 2, below_diagonal, 0)

    @pl.when(qi % 2 == 1)
    def _():
        step(qi - 1, False)

    step(qi, True)


def _attn_a_body(q_ref, k_ref, v_ref, km_ref, vm_ref, lamv_ref, g_ref, o_ref, m_scr, l_scr, acc_scr, p_scr,
                 *, lam_init):
    qi = pl.program_id(2)
    tq = q_ref.shape[0]
    wide_v = 2 * HEAD_DIM
    cols = lambda u: slice(u * HEAD_DIM, (u + 1) * HEAD_DIM)
    vcols = lambda u: slice((u // 2) * wide_v, (u // 2 + 1) * wide_v)

    for u in range(PROMPT_UNITS):
        p = _probs_init(_dot_nt(q_ref[:, cols(u)], km_ref[:, cols(u)]), m_scr.at[u])
        l_scr[u] = _lanes(jnp.sum(p, axis=1)[:, None])
        acc_scr[u] = _dot(p.astype(BF16), vm_ref[:, vcols(u)])
        p_scr[u] = jnp.zeros(p_scr.shape[1:], BF16)

    def step(kt, diagonal):
        prev = jnp.maximum(kt - 1, 0)
        for u in range(PROMPT_UNITS):
            s = _dot_nt(q_ref[:, cols(u)], _key_tile(k_ref, kt, tq, cols(u)))
            if diagonal:
                s = _diag_mask(s, CHUNK)
            flushed = acc_scr[u] + _dot(p_scr[u], _key_tile(v_ref, prev, tq, vcols(u)))
            m_new, alpha = _advance_max([s], m_scr.at[u])
            p = jnp.exp2(s - _wide(m_new, tq))
            l_scr[u] = alpha * l_scr[u] + _lanes(jnp.sum(p, axis=1)[:, None])
            p_scr[u] = p.astype(BF16)
            acc_scr[u] = _wide(alpha, wide_v) * flushed

    _pipelined_tiles(qi, step)
    lam = _lam(lamv_ref, lam_init)
    for h in range(PROMPT_UNITS // 2):
        o0, o1 = [(acc_scr[u] + _dot(p_scr[u], _key_tile(v_ref, qi, tq, vcols(u)))) / _wide(l_scr[u], wide_v)
                  for u in (2 * h, 2 * h + 1)]
        o_ref[:, vcols(2 * h)] = (_rms(o0 - lam * o1, g_ref[...]) * (1.0 - lam_init)).astype(BF16)


def _attn_b_body(q_ref, k_ref, v_ref, km_ref, vm_ref, nc_ref, ncm_ref, g_ref, o_ref, m_scr, acc_scr, p_scr):
    qi = pl.program_id(2)
    tq = q_ref.shape[0]
    cols = lambda u: slice(u * HEAD_DIM, (u + 1) * HEAD_DIM)

    for u in range(PROMPT_UNITS):
        s = _dot_nt(q_ref[:, cols(u)], km_ref[:, cols(u)]) + ncm_ref[u:u + 1, :] * LOG2E
        p = _probs_init(s, m_scr.at[u])
        acc_scr[u] = _dot(p.astype(BF16), _with_ones(vm_ref[:, cols(u)]))
        p_scr[u] = jnp.zeros(p_scr.shape[1:], BF16)

    def step(kt, diagonal):
        prev = jnp.maximum(kt - 1, 0)
        for u in range(PROMPT_UNITS):
            s = _dot_nt(q_ref[:, cols(u)], _key_tile(k_ref, kt, tq, cols(u))) + nc_ref[u, pl.ds(kt, 1), :] * LOG2E
            if diagonal:
                s = _diag_mask(s, 1)
            flushed = acc_scr[u] + _dot(p_scr[u], _with_ones(_key_tile(v_ref, prev, tq, cols(u))))
            m_new, alpha = _advance_max([s], m_scr.at[u])
            p_scr[u] = jnp.exp2(s - _wide(m_new, tq)).astype(BF16)
            acc_scr[u] = _wide(alpha, 2 * HEAD_DIM) * flushed

    _pipelined_tiles(qi, step)
    for u in range(PROMPT_UNITS):
        acc = acc_scr[u] + _dot(p_scr[u], _with_ones(_key_tile(v_ref, qi, tq, cols(u))))
        o_ref[:, cols(u)] = _rms(acc[:, :HEAD_DIM] / acc[:, HEAD_DIM:], g_ref[...]).astype(BF16)


def _attn_prompt(mode, q, k, v, km, vm, extra, g_o, *, batch, seq, lam_init=None):
    tq = _attn_tile(seq)
    nq = seq // tq
    q_spec = pl.BlockSpec((tq, PROMPT_GROUP_W), lambda b, h, i: (b * nq + i, h))
    kv_spec = pl.BlockSpec((seq, PROMPT_GROUP_W), lambda b, h, i: (b, h))
    meta_spec = pl.BlockSpec((N_META, PROMPT_GROUP_W), lambda b, h, i: (0, h))
    stat = pltpu.VMEM((PROMPT_UNITS, tq, LANES), F32)
    acc = pltpu.VMEM((PROMPT_UNITS, tq, 2 * HEAD_DIM), F32)
    pend = pltpu.VMEM((PROMPT_UNITS, tq, tq), BF16)
    if mode == "a":
        lamv, = extra
        body = functools.partial(_attn_a_body, lam_init=lam_init)
        extra_specs = [pl.BlockSpec(lamv.shape, lambda b, h, i: (0, 0))]
        scratch = [stat, stat, acc, pend]
    else:
        negc, negc_meta = extra
        body = _attn_b_body
        extra_specs = [pl.BlockSpec((None, None, PROMPT_UNITS, nq, tq), lambda b, h, i: (b, h, 0, 0, 0)),
                       pl.BlockSpec((None, PROMPT_UNITS, N_META), lambda b, h, i: (h, 0, 0))]
        scratch = [stat, acc, pend]
    g_spec = pl.BlockSpec((1, g_o.shape[-1]), lambda b, h, i: (0, 0))
    return pl.pallas_call(
        body,
        grid=(batch, PROMPT_GROUPS, nq),
        in_specs=[q_spec, kv_spec, kv_spec, meta_spec, meta_spec] + extra_specs + [g_spec],
        out_specs=q_spec,
        out_shape=jax.ShapeDtypeStruct((batch * seq, SEG), BF16),
        scratch_shapes=scratch,
        compiler_params=_params("parallel", "parallel", "arbitrary"),
        name="attn_prompt_" + mode,
    )(q, k, v, km, vm, *extra, g_o.reshape(1, -1))


def _slot_rows(ref, slot, n):
    return ref[pl.ds(slot, n, stride=ROW_SLOTS), :]


def _attn_sample_body(qa_ref, kan_ref, van_ref, cka_ref, cva_ref, qb_ref, kbn_ref, vbn_ref, ckb_ref, cvb_ref,
                      nco_ref, ncn_ref, lamv_ref, goa_ref, gob_ref, oa_ref, ob_ref,
                      m_scr, l_scr, acca_scr, accb_scr, *, lam_init):
    j = pl.program_id(1)
    t = qa_ref.shape[0]
    tb = cka_ref.shape[0] // ROW_SLOTS
    wide_v = 2 * HEAD_DIM
    cols = lambda u: slice(u * HEAD_DIM, (u + 1) * HEAD_DIM)
    n_a, n_b = 2 * A_HEADS, B_HEADS

    @pl.when(j == 0)
    def _():
        m_scr[...] = jnp.full(m_scr.shape, NEG, F32)
        l_scr[...] = jnp.zeros_like(l_scr)
        acca_scr[...] = jnp.zeros_like(acca_scr)
        accb_scr[...] = jnp.zeros_like(accb_scr)

    def update_a(u, s, v):
        m_new, alpha = _advance_max([s], m_scr.at[u])
        p = jnp.exp2(s - m_new[:, :1])
        l_scr[u] = alpha * l_scr[u] + _lanes(jnp.sum(p, axis=1)[:, None])
        acca_scr[u] = _wide(alpha, wide_v) * acca_scr[u] + _dot(p.astype(BF16), v)

    def update_b(h, s, v):
        m_new, alpha = _advance_max([s], m_scr.at[n_a + h])
        p = jnp.exp2(s - m_new[:, :1])
        accb_scr[h] = _wide(alpha, wide_v) * accb_scr[h] + _dot(p.astype(BF16), _with_ones(v))

    for h in range(A_HEADS):
        v = jnp.concatenate([_slot_rows(cva_ref, half * A_HEADS + h, tb) for half in range(2)], axis=1).astype(BF16)
        for m in range(2):
            u = 2 * h + m
            update_a(u, _dot_nt(qa_ref[:, cols(u)], _slot_rows(cka_ref, u, tb).astype(BF16)), v)
    for h in range(n_b):
        s = _dot_nt(qb_ref[:, cols(h)], _slot_rows(ckb_ref, h, tb).astype(BF16)) + nco_ref[h, pl.ds(j, 1), :] * LOG2E
        update_b(h, s, _slot_rows(cvb_ref, h, tb).astype(BF16))

    @pl.when(j == pl.num_programs(1) - 1)
    def _():
        lam = _lam(lamv_ref, lam_init)
        for h in range(A_HEADS):
            for m in range(2):
                u = 2 * h + m
                update_a(u, _dot_nt(qa_ref[:, cols(u)], kan_ref[:, cols(u)]), van_ref[:, h * wide_v:(h + 1) * wide_v])
            o0, o1 = [acca_scr[u] / _wide(l_scr[u], wide_v) for u in (2 * h, 2 * h + 1)]
            oa_ref[:, h * wide_v:(h + 1) * wide_v] = (_rms(o0 - lam * o1, goa_ref[...]) * (1.0 - lam_init)).astype(BF16)
        for h in range(n_b):
            s = _dot_nt(qb_ref[:, cols(h)], kbn_ref[:, cols(h)]) + ncn_ref[h:h + 1, :] * LOG2E
            update_b(h, _diag_mask(s, 1), vbn_ref[:, cols(h)])
            acc = accb_scr[h]
            ob_ref[:, cols(h)] = _rms(acc[:, :HEAD_DIM] / acc[:, HEAD_DIM:], gob_ref[...]).astype(BF16)


def _attn_sample(qa, kan, van, cka, cva, qb, kbn, vbn, ckb, cvb, negc_old, negc_new, lamv, g_oa, g_ob, *,
                 batch, t, past, lam_init):
    nblk, tb = negc_old.shape[-2:]
    new = pl.BlockSpec((t, SEG), lambda b, j: (b, 0))
    old = pl.BlockSpec((tb * ROW_SLOTS, HEAD_DIM), lambda b, j: (b * nblk + j, 0))
    whole = lambda a: pl.BlockSpec(a.shape, lambda b, j: (0,) * a.ndim)
    goa, gob = g_oa.reshape(1, -1), g_ob.reshape(1, -1)
    out = jax.ShapeDtypeStruct((batch * t, SEG), BF16)
    n_a = 2 * A_HEADS
    return pl.pallas_call(
        functools.partial(_attn_sample_body, lam_init=lam_init),
        grid=(batch, nblk),
        in_specs=[new, new, new, old, old, new, new, new, old, old,
                  pl.BlockSpec((None, B_HEADS, nblk, tb), lambda b, j: (b, 0, 0, 0)),
                  pl.BlockSpec((None, B_HEADS, t), lambda b, j: (b, 0, 0)),
                  whole(lamv), whole(goa), whole(gob)],
        out_specs=[new, new],
        out_shape=[out, out],
        scratch_shapes=[pltpu.VMEM((n_a + B_HEADS, t, LANES), F32), pltpu.VMEM((n_a, t, LANES), F32),
                        pltpu.VMEM((n_a, t, 2 * HEAD_DIM), F32), pltpu.VMEM((B_HEADS, t, 2 * HEAD_DIM), F32)],
        compiler_params=_params("parallel", "arbitrary"),
        name="attn_sample",
    )(qa, kan, van, cka, cva, qb, kbn, vbn, ckb, cvb, negc_old, negc_new, lamv, goa, gob)


def _merge_body(x_ref, oa_ref, ob_ref, wa_ref, wb_ref, o_ref):
    o_ref[...] = x_ref[...] + _dot(oa_ref[...], wa_ref[...]) + _dot(ob_ref[...], wb_ref[...])


def _merge(x, oa, ob, w_a, w_b):
    t, d = x.shape
    tm = _pick_tile(t, 512, 16)
    row = lambda w: pl.BlockSpec((tm, w), lambda i: (i, 0))
    wsp = pl.BlockSpec((SEG, d), lambda i: (0, 0))
    return pl.pallas_call(
        _merge_body,
        grid=(t // tm,),
        in_specs=[row(d), row(SEG), row(SEG), wsp, wsp],
        out_specs=row(d),
        out_shape=jax.ShapeDtypeStruct((t, d), F32),
        compiler_params=_params("parallel"),
        name="out_proj",
    )(x, oa, ob, w_a, w_b)


def kernel(x_prompt, x_sample, cache_a_k, cache_a_v, cache_b_k, cache_b_v, cache_b_logf, meta_tokens, g_ffn1, ffn1_w1, ffn1_w3, ffn1_w2, g_mix, w_in, b_f, g_qa, g_ka, g_qb, g_kb, lambda_q1, lambda_k1, lambda_q2, lambda_k2, g_oa, g_ob, w_out, g_ffn2, ffn2_w1, ffn2_w3, ffn2_w2, g_final):
    depth = w_in.shape[0]
    assert depth == 1, "meta rows skip attention, which is only valid for a single layer"
    bsz, seq, d = x_prompt.shape
    dbsz, dseq, _ = x_sample.shape
    past = cache_a_k.shape[2]
    n_small = dbsz * dseq
    lam_init = 0.8 - 0.6 * math.exp(-0.3 * 0)
    bf = lambda a: a.astype(BF16)

    w_main = bf(w_in[0, :, :N_SEG * SEG])
    w_f = bf(jnp.pad(w_in[0, :, N_SEG * SEG:], ((0, 0), (0, LANES - B_HEADS))))
    b_fp = jnp.pad(b_f[0], (0, LANES - B_HEADS)).reshape(1, LANES)
    lamv = jnp.stack([lambda_q1[0], lambda_k1[0], lambda_q2[0], lambda_k2[0]]).astype(F32)
    wo_a, wo_b = bf(w_out[0, :SEG]), bf(w_out[0, SEG:])
    f1 = (g_ffn1[0], bf(ffn1_w1[0]), bf(ffn1_w3[0]), bf(ffn1_w2[0]), g_mix[0])
    f2 = (g_ffn2[0], bf(ffn2_w1[0]), bf(ffn2_w3[0]), bf(ffn2_w2[0]), g_final[0])
    gains = (g_qa[0], g_ka[0], g_qb[0], g_kb[0])

    xp = x_prompt.reshape(bsz * seq, d)
    n_pad = -(n_small + N_META) % SMALL_ROW_MULT
    xs = jnp.concatenate([x_sample.reshape(n_small, d), meta_tokens.astype(x_sample.dtype),
                          jnp.zeros((n_pad, d), x_sample.dtype)], axis=0)
    pos_p = N_META + jnp.arange(seq)
    pos_s = jnp.concatenate([jnp.tile(past + jnp.arange(dseq), dbsz), jnp.arange(N_META),
                             jnp.zeros((n_pad,), jnp.int32)])

    x1p, hp = _ffn(xp, *f1, mode="emit_norm")
    x1s, hs = _ffn(xs, *f1, mode="emit_norm")
    (qa_p, kaf_p, kab_p, vaf_p, vab_p, qb_p, kbf_p, kbb_p, vbf_p, vbb_p, logf_p) = _project(
        hp, w_main, w_f, b_fp, *gains, pos_p, rows_per_seq=seq, gap=N_META)
    (qa_s, kaf_s, kab_s, vaf_s, vab_s, qb_s, kbf_s, kbb_s, vbf_s, vbb_s, logf_s) = _project(
        hs, w_main, w_f, b_fp, *gains, pos_s, rows_per_seq=xs.shape[0])

    logf_meta = logf_s[n_small:n_small + N_META]
    c_real = _cumsum_lanes(logf_p.reshape(bsz, seq, B_HEADS).transpose(0, 2, 1).reshape(bsz * B_HEADS, seq))
    c_meta = _cumsum_lanes(logf_meta.T)
    seq_s = jnp.concatenate([cache_b_logf[0].astype(F32), logf_s[:n_small].reshape(dbsz, dseq, B_HEADS)], axis=1)
    c_small = _cumsum_lanes(seq_s.transpose(0, 2, 1).reshape(dbsz * B_HEADS, past + dseq))
    tq = _attn_tile(seq)
    negc_p = (-c_real).reshape(bsz, PROMPT_GROUPS, PROMPT_UNITS, seq // tq, tq)
    negc_meta = (c_meta[:, N_META - 1:] - c_meta).reshape(PROMPT_GROUPS, PROMPT_UNITS, N_META)
    tb = _pick_tile(past, SAMPLE_BLOCK, LANES)
    negc_s_old = (-c_small[:, :past]).reshape(dbsz, B_HEADS, past // tb, tb)
    negc_s_new = (-c_small[:, past:]).reshape(dbsz, B_HEADS, dseq)

    meta = slice(n_small, n_small + N_META)
    oa_p = _attn_prompt("a", qa_p, kab_p, vab_p, kab_s[meta], vab_s[meta], (lamv,), g_oa[0],
                        batch=bsz, seq=seq, lam_init=lam_init)
    ob_p = _attn_prompt("b", qb_p, kbb_p, vbb_p, kbb_s[meta], vbb_s[meta], (negc_p, negc_meta), g_ob[0],
                        batch=bsz, seq=seq)
    rows = lambda c: c[0].reshape(-1, HEAD_DIM)
    cva = cache_a_v[0].reshape(dbsz, past, A_HEADS, 2, HEAD_DIM).swapaxes(2, 3).reshape(-1, HEAD_DIM)
    oa_s, ob_s = _attn_sample(
        qa_s, kab_s, vab_s, rows(cache_a_k), cva, qb_s, kbb_s, vbb_s, rows(cache_b_k), rows(cache_b_v),
        negc_s_old, negc_s_new, lamv, g_oa[0], g_ob[0], batch=dbsz, t=dseq, past=past, lam_init=lam_init)

    y_p = _ffn(_merge(x1p, oa_p, ob_p, wo_a, wo_b), *f2, mode="final_norm")
    y_s = _ffn(_merge(x1s[:n_small], oa_s, ob_s, wo_a, wo_b), *f2, mode="final_norm")

    def unslot(x, lead, tail, slots=None):
        if slots == VA_SLOTS:
            x = x.reshape(lead + (2, A_HEADS, HEAD_DIM)).swapaxes(-3, -2)
        return x.reshape(lead + tail)

    small_f32 = (kaf_s, vaf_s, kbf_s, vbf_s)
    meta_rows = [x[n_small * ROW_SLOTS:(n_small + N_META) * ROW_SLOTS] for x in small_f32]
    full = _fill_gaps((kaf_p, vaf_p, kbf_p, vbf_p), meta_rows, n_seq=bsz)
    tails = ((A_HEADS, 2, HEAD_DIM), (A_HEADS, 2 * HEAD_DIM), (B_HEADS, HEAD_DIM), (B_HEADS, HEAD_DIM))
    slots = (None, VA_SLOTS, None, None)
    ak_p, av_p, bk_p, bv_p = [unslot(x, (1, bsz, N_META + seq), tl, sl) for x, tl, sl in zip(full, tails, slots)]
    ak_s, av_s, bk_s, bv_s = [unslot(x[:n_small * ROW_SLOTS], (1, dbsz, dseq), tl, sl)
                              for x, tl, sl in zip(small_f32, tails, slots)]
    lf_meta = jnp.broadcast_to(logf_meta[None], (bsz, N_META, B_HEADS))
    lf_p = jnp.concatenate([lf_meta, logf_p.reshape(bsz, seq, B_HEADS)], axis=1)[None]
    lf_s = logf_s[:n_small].reshape(1, dbsz, dseq, B_HEADS)
    return (y_p.reshape(bsz, seq, d), y_s.reshape(dbsz, dseq, d),
            ak_p, av_p, bk_p, bv_p, lf_p, ak_s, av_s, bk_s, bv_s, lf_s)
```

```python
import functools
import math

import jax
import jax.numpy as jnp
import numpy as np
from jax import lax
from jax.experimental import pallas as pl
from jax.experimental.pallas import tpu as pltpu

F32 = jnp.float32
BF16 = jnp.bfloat16

HEAD_DIM = 128
A_HEADS = 4
B_HEADS = 8
SEG = 1024
N_SEG = 6
PROMPT_UNITS = 4
PROMPT_GROUP_W = PROMPT_UNITS * HEAD_DIM
PROMPT_GROUPS = SEG // PROMPT_GROUP_W
ROW_SLOTS = SEG // HEAD_DIM
VA_SLOTS = [(c % 2) * A_HEADS + c // 2 for c in range(ROW_SLOTS)]
N_META = 16
CHUNK = 64
ROPE_DIM = HEAD_DIM // 4
ROPE_THETA = 500000.0
EPS = 1e-6
NEG = -1e30
LOG2E = math.log2(math.e)
LANES = 128
VMEM_LIMIT = 60 * 1024 * 1024
SAMPLE_BLOCK = 512
SMALL_ROW_MULT = 512


def _params(*semantics):
    return pltpu.CompilerParams(dimension_semantics=semantics, vmem_limit_bytes=VMEM_LIMIT)


def _pick_tile(n, target, mult):
    best = None
    for t in range(mult, min(n, target) + 1, mult):
        if n % t == 0:
            best = t
    assert best is not None, (n, target, mult)
    return best


def _rms(x, g):
    ms = jnp.mean(x * x, axis=-1, keepdims=True)
    return (x * lax.rsqrt(ms + EPS)) * g


def _dot(a, b):
    return jnp.dot(a, b, preferred_element_type=F32)


def _dot_nt(a, b):
    return lax.dot_general(a, b, (((1,), (1,)), ((), ())), preferred_element_type=F32)


def _ffn_body(x_ref, g_ref, w1_ref, w3_ref, w2_ref, gout_ref, *rest, mode, n_tiles):
    if mode == "emit_norm":
        out_ref, h_ref, xn_scr, acc_scr = rest
    else:
        out_ref, xn_scr, acc_scr = rest
    i, f = pl.program_id(0), pl.program_id(1)

    def partial():
        xn = xn_scr[...]
        h1 = _dot(xn, w1_ref[...])
        h3 = _dot(xn, w3_ref[...])
        gate = (h1 * jax.nn.sigmoid(h1)) * h3
        return _dot(gate.astype(BF16), w2_ref[...])

    def finish(slot):
        y = 0.5 * acc_scr[slot]
        if mode == "emit_norm":
            out_ref[...] = y
            h_ref[...] = _rms(y, gout_ref[...]).astype(BF16)
        else:
            out_ref[...] = _rms(y, gout_ref[...])

    @pl.when(jnp.logical_and(i == 0, f == 0))
    def _():
        acc_scr[1] = jnp.zeros(acc_scr.shape[1:], F32)

    for slot in range(2):
        mine = jnp.logical_and(i < n_tiles, i % 2 == slot)

        @pl.when(jnp.logical_and(mine, f == 0))
        def _(slot=slot):
            x = x_ref[...]
            xn_scr[...] = _rms(x, g_ref[...]).astype(BF16)
            acc_scr[slot] = 2.0 * x + partial()
            finish(1 - slot)

        @pl.when(jnp.logical_and(mine, f > 0))
        def _(slot=slot):
            acc_scr[slot] += partial()

    @pl.when(jnp.logical_and(i == n_tiles, f == 0))
    def _():
        finish((n_tiles - 1) % 2)


def _ffn(x, g_in, w1, w3, w2, g_out, *, mode):
    t, d = x.shape
    f = w1.shape[1]
    tm = _pick_tile(t, 512, 16)
    tf = _pick_tile(f, 512, LANES)
    n_tiles, nf = t // tm, f // tf
    last = n_tiles - 1
    f_idx = lambda i, j: jnp.where(i < n_tiles, j, nf - 1)
    x_spec = pl.BlockSpec((tm, d), lambda i, j: (jnp.minimum(i, last), 0))
    out_spec = pl.BlockSpec((tm, d), lambda i, j: (jnp.where(j == 0, jnp.maximum(i - 1, 0), jnp.minimum(i, last)), 0))
    vec = pl.BlockSpec((1, d), lambda i, j: (0, 0))
    out_shape = [jax.ShapeDtypeStruct((t, d), F32)]
    out_specs = [out_spec]
    if mode == "emit_norm":
        out_shape.append(jax.ShapeDtypeStruct((t, d), BF16))
        out_specs.append(out_spec)
    res = pl.pallas_call(
        functools.partial(_ffn_body, mode=mode, n_tiles=n_tiles),
        grid=(n_tiles + 1, nf),
        in_specs=[x_spec, vec,
                  pl.BlockSpec((d, tf), lambda i, j: (0, f_idx(i, j))),
                  pl.BlockSpec((d, tf), lambda i, j: (0, f_idx(i, j))),
                  pl.BlockSpec((tf, d), lambda i, j: (f_idx(i, j), 0)),
                  vec],
        out_specs=out_specs,
        out_shape=out_shape,
        scratch_shapes=[pltpu.VMEM((tm, d), BF16), pltpu.VMEM((2, tm, d), F32)],
        compiler_params=_params("arbitrary", "arbitrary"),
        name="ffn_" + mode,
    )(x, g_in.reshape(1, d), w1, w3, w2, g_out.reshape(1, d))
    return res if mode == "emit_norm" else res[0]


def _proj_body(h_ref, w_ref, wf_ref, bf_ref, gqa_ref, gka_ref, gqb_ref, gkb_ref, tc_ref, ta_ref, tb_ref,
               qa_ref, kaf_ref, kab_ref, vaf_ref, vab_ref, qb_ref, kbf_ref, kbb_ref, vbf_ref, vbb_ref,
               logf_ref, acc_scr):
    j = pl.program_id(1)
    q_scale = HEAD_DIM ** -0.5 * LOG2E
    natural = list(range(ROW_SLOTS))
    segments = [(gqa_ref, True, None, None, qa_ref, q_scale),
                (gka_ref, True, kaf_ref, natural, kab_ref, None),
                (None, False, vaf_ref, VA_SLOTS, vab_ref, None),
                (gqb_ref, False, None, None, qb_ref, q_scale),
                (gkb_ref, False, kbf_ref, natural, kbb_ref, None),
                (None, False, vbf_ref, natural, vbb_ref, None)]

    def finish(k):
        g_ref, rope, f32_ref, slots, bf_ref_, scale = segments[k]
        for c in range(ROW_SLOTS):
            sl = slice(c * HEAD_DIM, (c + 1) * HEAD_DIM)
            y = acc_scr[k % 2, :, sl]
            if g_ref is not None:
                y = _rms(y, g_ref[...])
            if rope:
                y = (y * tc_ref[...] + pltpu.roll(y, HEAD_DIM - ROPE_DIM // 2, 1) * ta_ref[...]
                     + pltpu.roll(y, ROPE_DIM // 2, 1) * tb_ref[...])
            if f32_ref is not None:
                f32_ref[pl.ds(slots[c], y.shape[0], stride=ROW_SLOTS), :] = y
            bf_ref_[:, sl] = (y if scale is None else y * scale).astype(BF16)

    for k in range(N_SEG + 1):
        @pl.when(j == k)
        def _(k=k):
            if k < N_SEG:
                acc_scr[k % 2] = _dot(h_ref[...], w_ref[...])
            if k == 0:
                z = _dot(h_ref[...], wf_ref[...]) + bf_ref[...]
                logf = jnp.minimum(z, 0.0) - jnp.log1p(jnp.exp(-jnp.abs(z)))
                logf_ref[...] = logf[:, :B_HEADS]
            else:
                finish(k - 1)


def _rope_tables(pos):
    half = ROPE_DIM // 2
    inv = jnp.power(ROPE_THETA, -jnp.arange(half, dtype=F32) * 2.0 / ROPE_DIM)
    ang = pos.astype(F32)[:, None] * inv[None, :]
    cos, sin = jnp.cos(ang), jnp.sin(ang)
    n = pos.shape[0]
    zeros = jnp.zeros((n, HEAD_DIM - ROPE_DIM), F32)
    tab_c = jnp.concatenate([cos, cos, jnp.ones((n, HEAD_DIM - ROPE_DIM), F32)], axis=1)
    tab_a = jnp.concatenate([-sin, jnp.zeros((n, half), F32), zeros], axis=1)
    tab_b = jnp.concatenate([jnp.zeros((n, half), F32), sin, zeros], axis=1)
    return tab_c, tab_a, tab_b


def _project(h, w_main, w_f, b_f, g_qa, g_ka, g_qb, g_kb, pos, *, rows_per_seq, gap=0):
    t, d = h.shape
    tm = _pick_tile(rows_per_seq, 512, 16)
    tiles_per_seq = rows_per_seq // tm
    tabs = _rope_tables(pos)
    row = lambda w: pl.BlockSpec((tm, w), lambda i, j: (i, 0))
    n_tiles = t // tm
    first_row = lambda i: ((i // tiles_per_seq) * (gap + rows_per_seq) + gap + (i % tiles_per_seq) * tm) * ROW_SLOTS

    def tile_at(done):
        return lambda i, j: jnp.minimum(i + (j > done).astype(jnp.int32), n_tiles - 1)

    wide = lambda done: pl.BlockSpec((tm, SEG), lambda i, j: (tile_at(done)(i, j), 0))
    tall = lambda done: pl.BlockSpec((pl.Element(tm * ROW_SLOTS), pl.Element(HEAD_DIM)),
                                     lambda i, j: (first_row(tile_at(done)(i, j)), 0))
    logf_spec = pl.BlockSpec((tm, B_HEADS), lambda i, j: (tile_at(0)(i, j), 0))
    gain = pl.BlockSpec((1, HEAD_DIM), lambda i, j: (0, 0))
    tab = pl.BlockSpec((tm, HEAD_DIM), lambda i, j: (i % tiles_per_seq, 0))
    bf = jax.ShapeDtypeStruct((t, SEG), BF16)
    f32 = jax.ShapeDtypeStruct(((t // rows_per_seq) * (gap + rows_per_seq) * ROW_SLOTS, HEAD_DIM), F32)
    return pl.pallas_call(
        _proj_body,
        grid=(t // tm, N_SEG + 1),
        in_specs=[row(d),
                  pl.BlockSpec((d, SEG), lambda i, j: (0, jnp.minimum(j, N_SEG - 1))),
                  pl.BlockSpec((d, LANES), lambda i, j: (0, 0)),
                  pl.BlockSpec((1, LANES), lambda i, j: (0, 0)),
                  gain, gain, gain, gain, tab, tab, tab],
        out_specs=[wide(1), tall(2), wide(2), tall(3), wide(3), wide(4), tall(5), wide(5), tall(6), wide(6), logf_spec],
        out_shape=[bf, f32, bf, f32, bf, bf, f32, bf, f32, bf, jax.ShapeDtypeStruct((t, B_HEADS), F32)],
        scratch_shapes=[pltpu.VMEM((2, tm, SEG), F32)],
        compiler_params=_params("arbitrary", "arbitrary"),
        name="in_proj",
    )(h, w_main, w_f, b_f, g_qa.reshape(1, -1), g_ka.reshape(1, -1), g_qb.reshape(1, -1),
      g_kb.reshape(1, -1), *tabs)


def _fill_body(*refs):
    n = len(refs) // 3
    for src_ref, out_ref in zip(refs[n:2 * n], refs[2 * n:]):
        out_ref[...] = src_ref[...]


def _fill_gaps(dsts, srcs, *, n_seq):
    gap_rows = srcs[0].shape[0]
    blocks_per_seq = dsts[0].shape[0] // n_seq // gap_rows
    assert blocks_per_seq * gap_rows * n_seq == dsts[0].shape[0]
    n = len(dsts)
    return pl.pallas_call(
        _fill_body,
        grid=(n_seq,),
        in_specs=[pl.BlockSpec(memory_space=pl.ANY)] * n + [pl.BlockSpec(srcs[0].shape, lambda b: (0, 0))] * n,
        out_specs=[pl.BlockSpec((gap_rows, HEAD_DIM), lambda b: (b * blocks_per_seq, 0))] * n,
        out_shape=[jax.ShapeDtypeStruct(x.shape, x.dtype) for x in dsts],
        input_output_aliases={k: k for k in range(n)},
        compiler_params=_params("parallel"),
        name="fill_meta_rows",
    )(*dsts, *srcs)


def _cumsum_body(x_ref, upper_ref, earlier_ref, o_ref):
    sb, nb, _ = x_ref.shape
    hi = lax.Precision.HIGHEST
    x = x_ref[...].reshape(sb * nb, LANES)
    within = jnp.dot(x, upper_ref[...], precision=hi, preferred_element_type=F32)
    totals = jnp.broadcast_to(within[:, LANES - 1:LANES], within.shape)
    carry = jnp.dot(earlier_ref[...], totals, precision=hi, preferred_element_type=F32)
    o_ref[...] = (within + carry).reshape(sb, nb, LANES)


def _cumsum_lanes(x):
    s, length = x.shape
    lp = -(-length // (8 * LANES)) * (8 * LANES)
    nb = lp // LANES
    sb = _pick_tile(s, max(1, 512 // nb), 1)
    r = sb * nb
    xp = jnp.pad(x, ((0, 0), (0, lp - length))).reshape(s, nb, LANES)
    idx = np.arange(LANES)
    upper = jnp.asarray(idx[:, None] <= idx[None, :], F32)
    rid = np.arange(r)
    earlier = jnp.asarray((rid[None, :] < rid[:, None]) & (rid[None, :] // nb == rid[:, None] // nb), F32)
    blk = pl.BlockSpec((sb, nb, LANES), lambda i: (i, 0, 0))
    out = pl.pallas_call(
        _cumsum_body,
        grid=(s // sb,),
        in_specs=[blk, pl.BlockSpec((LANES, LANES), lambda i: (0, 0)), pl.BlockSpec((r, r), lambda i: (0, 0))],
        out_specs=blk,
        out_shape=jax.ShapeDtypeStruct((s, nb, LANES), F32),
        compiler_params=_params("parallel"),
        name="cumsum",
    )(xp, upper, earlier)
    return out.reshape(s, lp)[:, :length]


def _lam(lamv_ref, lam_init):
    v = lamv_ref[...]
    s1 = jnp.sum(v[0:1] * v[1:2], axis=-1, keepdims=True)
    s2 = jnp.sum(v[2:3] * v[3:4], axis=-1, keepdims=True)
    return jnp.exp(s1) - jnp.exp(s2) + lam_init


def _lanes(col):
    return jnp.broadcast_to(col, (col.shape[0], LANES))


def _wide(stat, width):
    return jnp.tile(stat, (1, width // LANES))


def _with_ones(v):
    return jnp.concatenate([v, jnp.ones((v.shape[0], LANES), v.dtype)], axis=1)


def _advance_max(ss, m_ref):
    m_prev = m_ref[...]
    m_new = m_prev
    for s in ss:
        m_new = jnp.maximum(m_new, jnp.max(s, axis=1)[:, None])
    m_ref[...] = m_new
    return m_new, jnp.exp2(m_prev - m_new)


def _diag_mask(s, chunk):
    rows = lax.broadcasted_iota(jnp.int32, s.shape, 0) // chunk
    cols = lax.broadcasted_iota(jnp.int32, s.shape, 1) // chunk
    return jnp.where(cols <= rows, s, NEG)


def _probs_init(s, m_ref):
    m = jnp.max(s, axis=1)[:, None]
    m_ref[...] = _lanes(m)
    return jnp.exp2(s - m)


def _attn_tile(seq):
    return _pick_tile(seq, 512, CHUNK)


def _key_tile(ref, kt, tk, cols):
    return ref[pl.ds(pl.multiple_of(kt * tk, tk), tk), cols]


def _pipelined_tiles(qi, step):
    def below_diagonal(i, carry):
        step(2 * i, False)
        step(2 * i + 1, False)
        return carry

    lax.fori_loop(0, qi // 2, below_diagonal, 0)

    @pl.when(qi % 2 == 1)
    def _():
        step(qi - 1, False)

    step(qi, True)


def _attn_a_body(q_ref, k_ref, v_ref, km_ref, vm_ref, lamv_ref, g_ref, o_ref, m_scr, l_scr, acc_scr, p_scr,
                 *, lam_init):
    qi = pl.program_id(2)
    tq = q_ref.shape[0]
    wide_v = 2 * HEAD_DIM
    cols = lambda u: slice(u * HEAD_DIM, (u + 1) * HEAD_DIM)
    vcols = lambda u: slice((u // 2) * wide_v, (u // 2 + 1) * wide_v)

    for u in range(PROMPT_UNITS):
        p = _probs_init(_dot_nt(q_ref[:, cols(u)], km_ref[:, cols(u)]), m_scr.at[u])
        l_scr[u] = _lanes(jnp.sum(p, axis=1)[:, None])
        acc_scr[u] = _dot(p.astype(BF16), vm_ref[:, vcols(u)])
        p_scr[u] = jnp.zeros(p_scr.shape[1:], BF16)

    def step(kt, diagonal):
        prev = jnp.maximum(kt - 1, 0)
        for u in range(PROMPT_UNITS):
            s = _dot_nt(q_ref[:, cols(u)], _key_tile(k_ref, kt, tq, cols(u)))
            if diagonal:
                s = _diag_mask(s, CHUNK)
            flushed = acc_scr[u] + _dot(p_scr[u], _key_tile(v_ref, prev, tq, vcols(u)))
            m_new, alpha = _advance_max([s], m_scr.at[u])
            p = jnp.exp2(s - _wide(m_new, tq))
            l_scr[u] = alpha * l_scr[u] + _lanes(jnp.sum(p, axis=1)[:, None])
            p_scr[u] = p.astype(BF16)
            acc_scr[u] = _wide(alpha, wide_v) * flushed

    _pipelined_tiles(qi, step)
    lam = _lam(lamv_ref, lam_init)
    for h in range(PROMPT_UNITS // 2):
        o0, o1 = [(acc_scr[u] + _dot(p_scr[u], _key_tile(v_ref, qi, tq, vcols(u)))) / _wide(l_scr[u], wide_v)
                  for u in (2 * h, 2 * h + 1)]
        o_ref[:, vcols(2 * h)] = (_rms(o0 - lam * o1, g_ref[...]) * (1.0 - lam_init)).astype(BF16)


def _attn_b_body(q_ref, k_ref, v_ref, km_ref, vm_ref, nc_ref, ncm_ref, g_ref, o_ref, m_scr, acc_scr, p_scr):
    qi = pl.program_id(2)
    tq = q_ref.shape[0]
    cols = lambda u: slice(u * HEAD_DIM, (u + 1) * HEAD_DIM)

    for u in range(PROMPT_UNITS):
        s = _dot_nt(q_ref[:, cols(u)], km_ref[:, cols(u)]) + ncm_ref[u:u + 1, :] * LOG2E
        p = _probs_init(s, m_scr.at[u])
        acc_scr[u] = _dot(p.astype(BF16), _with_ones(vm_ref[:, cols(u)]))
        p_scr[u] = jnp.zeros(p_scr.shape[1:], BF16)

    def step(kt, diagonal):
        prev = jnp.maximum(kt - 1, 0)
        for u in range(PROMPT_UNITS):
            s = _dot_nt(q_ref[:, cols(u)], _key_tile(k_ref, kt, tq, cols(u))) + nc_ref[u, pl.ds(kt, 1), :] * LOG2E
            if diagonal:
                s = _diag_mask(s, 1)
            flushed = acc_scr[u] + _dot(p_scr[u], _with_ones(_key_tile(v_ref, prev, tq, cols(u))))
            m_new, alpha = _advance_max([s], m_scr.at[u])
            p_scr[u] = jnp.exp2(s - _wide(m_new, tq)).astype(BF16)
            acc_scr[u] = _wide(alpha, 2 * HEAD_DIM) * flushed

    _pipelined_tiles(qi, step)
    for u in range(PROMPT_UNITS):
        acc = acc_scr[u] + _dot(p_scr[u], _with_ones(_key_tile(v_ref, qi, tq, cols(u))))
        o_ref[:, cols(u)] = _rms(acc[:, :HEAD_DIM] / acc[:, HEAD_DIM:], g_ref[...]).astype(BF16)


def _attn_prompt(mode, q, k, v, km, vm, extra, g_o, *, batch, seq, lam_init=None):
    tq = _attn_tile(seq)
    nq = seq // tq
    q_spec = pl.BlockSpec((tq, PROMPT_GROUP_W), lambda b, h, i: (b * nq + i, h))
    kv_spec = pl.BlockSpec((seq, PROMPT_GROUP_W), lambda b, h, i: (b, h))
    meta_spec = pl.BlockSpec((N_META, PROMPT_GROUP_W), lambda b, h, i: (0, h))
    stat = pltpu.VMEM((PROMPT_UNITS, tq, LANES), F32)
    acc = pltpu.VMEM((PROMPT_UNITS, tq, 2 * HEAD_DIM), F32)
    pend = pltpu.VMEM((PROMPT_UNITS, tq, tq), BF16)
    if mode == "a":
        lamv, = extra
        body = functools.partial(_attn_a_body, lam_init=lam_init)
        extra_specs = [pl.BlockSpec(lamv.shape, lambda b, h, i: (0, 0))]
        scratch = [stat, stat, acc, pend]
    else:
        negc, negc_meta = extra
        body = _attn_b_body
        extra_specs = [pl.BlockSpec((None, None, PROMPT_UNITS, nq, tq), lambda b, h, i: (b, h, 0, 0, 0)),
                       pl.BlockSpec((None, PROMPT_UNITS, N_META), lambda b, h, i: (h, 0, 0))]
        scratch = [stat, acc, pend]
    g_spec = pl.BlockSpec((1, g_o.shape[-1]), lambda b, h, i: (0, 0))
    return pl.pallas_call(
        body,
        grid=(batch, PROMPT_GROUPS, nq),
        in_specs=[q_spec, kv_spec, kv_spec, meta_spec, meta_spec] + extra_specs + [g_spec],
        out_specs=q_spec,
        out_shape=jax.ShapeDtypeStruct((batch * seq, SEG), BF16),
        scratch_shapes=scratch,
        compiler_params=_params("parallel", "parallel", "arbitrary"),
        name="attn_prompt_" + mode,
    )(q, k, v, km, vm, *extra, g_o.reshape(1, -1))


def _slot_rows(ref, slot, n):
    return ref[pl.ds(slot, n, stride=ROW_SLOTS), :]


def _attn_sample_body(qa_ref, kan_ref, van_ref, cka_ref, cva_ref, qb_ref, kbn_ref, vbn_ref, ckb_ref, cvb_ref,
                      nco_ref, ncn_ref, lamv_ref, goa_ref, gob_ref, oa_ref, ob_ref,
                      m_scr, l_scr, acca_scr, accb_scr, *, lam_init):
    j = pl.program_id(1)
    t = qa_ref.shape[0]
    tb = cka_ref.shape[0] // ROW_SLOTS
    wide_v = 2 * HEAD_DIM
    cols = lambda u: slice(u * HEAD_DIM, (u + 1) * HEAD_DIM)
    n_a, n_b = 2 * A_HEADS, B_HEADS

    @pl.when(j == 0)
    def _():
        m_scr[...] = jnp.full(m_scr.shape, NEG, F32)
        l_scr[...] = jnp.zeros_like(l_scr)
        acca_scr[...] = jnp.zeros_like(acca_scr)
        accb_scr[...] = jnp.zeros_like(accb_scr)

    def update_a(u, s, v):
        m_new, alpha = _advance_max([s], m_scr.at[u])
        p = jnp.exp2(s - m_new[:, :1])
        l_scr[u] = alpha * l_scr[u] + _lanes(jnp.sum(p, axis=1)[:, None])
        acca_scr[u] = _wide(alpha, wide_v) * acca_scr[u] + _dot(p.astype(BF16), v)

    def update_b(h, s, v):
        m_new, alpha = _advance_max([s], m_scr.at[n_a + h])
        p = jnp.exp2(s - m_new[:, :1])
        accb_scr[h] = _wide(alpha, wide_v) * accb_scr[h] + _dot(p.astype(BF16), _with_ones(v))

    for h in range(A_HEADS):
        v = jnp.concatenate([_slot_rows(cva_ref, half * A_HEADS + h, tb) for half in range(2)], axis=1).astype(BF16)
        for m in range(2):
            u = 2 * h + m
            update_a(u, _dot_nt(qa_ref[:, cols(u)], _slot_rows(cka_ref, u, tb).astype(BF16)), v)
    for h in range(n_b):
        s = _dot_nt(qb_ref[:, cols(h)], _slot_rows(ckb_ref, h, tb).astype(BF16)) + nco_ref[h, pl.ds(j, 1), :] * LOG2E
        update_b(h, s, _slot_rows(cvb_ref, h, tb).astype(BF16))

    @pl.when(j == pl.num_programs(1) - 1)
    def _():
        lam = _lam(lamv_ref, lam_init)
        for h in range(A_HEADS):
            for m in range(2):
                u = 2 * h + m
                update_a(u, _dot_nt(qa_ref[:, cols(u)], kan_ref[:, cols(u)]), van_ref[:, h * wide_v:(h + 1) * wide_v])
            o0, o1 = [acca_scr[u] / _wide(l_scr[u], wide_v) for u in (2 * h, 2 * h + 1)]
            oa_ref[:, h * wide_v:(h + 1) * wide_v] = (_rms(o0 - lam * o1, goa_ref[...]) * (1.0 - lam_init)).astype(BF16)
        for h in range(n_b):
            s = _dot_nt(qb_ref[:, cols(h)], kbn_ref[:, cols(h)]) + ncn_ref[h:h + 1, :] * LOG2E
            update_b(h, _diag_mask(s, 1), vbn_ref[:, cols(h)])
            acc = accb_scr[h]
            ob_ref[:, cols(h)] = _rms(acc[:, :HEAD_DIM] / acc[:, HEAD_DIM:], gob_ref[...]).astype(BF16)


def _attn_sample(qa, kan, van, cka, cva, qb, kbn, vbn, ckb, cvb, negc_old, negc_new, lamv, g_oa, g_ob, *,
                 batch, t, past, lam_init):
    nblk, tb = negc_old.shape[-2:]
    new = pl.BlockSpec((t, SEG), lambda b, j: (b, 0))
    old = pl.BlockSpec((tb * ROW_SLOTS, HEAD_DIM), lambda b, j: (b * nblk + j, 0))
    whole = lambda a: pl.BlockSpec(a.shape, lambda b, j: (0,) * a.ndim)
    goa, gob = g_oa.reshape(1, -1), g_ob.reshape(1, -1)
    out = jax.ShapeDtypeStruct((batch * t, SEG), BF16)
    n_a = 2 * A_HEADS
    return pl.pallas_call(
        functools.partial(_attn_sample_body, lam_init=lam_init),
        grid=(batch, nblk),
        in_specs=[new, new, new, old, old, new, new, new, old, old,
                  pl.BlockSpec((None, B_HEADS, nblk, tb), lambda b, j: (b, 0, 0, 0)),
                  pl.BlockSpec((None, B_HEADS, t), lambda b, j: (b, 0, 0)),
                  whole(lamv), whole(goa), whole(gob)],
        out_specs=[new, new],
        out_shape=[out, out],
        scratch_shapes=[pltpu.VMEM((n_a + B_HEADS, t, LANES), F32), pltpu.VMEM((n_a, t, LANES), F32),
                        pltpu.VMEM((n_a, t, 2 * HEAD_DIM), F32), pltpu.VMEM((B_HEADS, t, 2 * HEAD_DIM), F32)],
        compiler_params=_params("parallel", "arbitrary"),
        name="attn_sample",
    )(qa, kan, van, cka, cva, qb, kbn, vbn, ckb, cvb, negc_old, negc_new, lamv, goa, gob)


def _merge_body(x_ref, oa_ref, ob_ref, wa_ref, wb_ref, o_ref):
    o_ref[...] = x_ref[...] + _dot(oa_ref[...], wa_ref[...]) + _dot(ob_ref[...], wb_ref[...])


def _merge(x, oa, ob, w_a, w_b):
    t, d = x.shape
    tm = _pick_tile(t, 512, 16)
    row = lambda w: pl.BlockSpec((tm, w), lambda i: (i, 0))
    wsp = pl.BlockSpec((SEG, d), lambda i: (0, 0))
    return pl.pallas_call(
        _merge_body,
        grid=(t // tm,),
        in_specs=[row(d), row(SEG), row(SEG), wsp, wsp],
        out_specs=row(d),
        out_shape=jax.ShapeDtypeStruct((t, d), F32),
        compiler_params=_params("parallel"),
        name="out_proj",
    )(x, oa, ob, w_a, w_b)


def kernel(x_prompt, x_sample, cache_a_k, cache_a_v, cache_b_k, cache_b_v, cache_b_logf, meta_tokens, g_ffn1, ffn1_w1, ffn1_w3, ffn1_w2, g_mix, w_in, b_f, g_qa, g_ka, g_qb, g_kb, lambda_q1, lambda_k1, lambda_q2, lambda_k2, g_oa, g_ob, w_out, g_ffn2, ffn2_w1, ffn2_w3, ffn2_w2, g_final):
    depth = w_in.shape[0]
    assert depth == 1, "meta rows skip attention, which is only valid for a single layer"
    bsz, seq, d = x_prompt.shape
    dbsz, dseq, _ = x_sample.shape
    past = cache_a_k.shape[2]
    n_small = dbsz * dseq
    lam_init = 0.8 - 0.6 * math.exp(-0.3 * 0)
    bf = lambda a: a.astype(BF16)

    w_main = bf(w_in[0, :, :N_SEG * SEG])
    w_f = bf(jnp.pad(w_in[0, :, N_SEG * SEG:], ((0, 0), (0, LANES - B_HEADS))))
    b_fp = jnp.pad(b_f[0], (0, LANES - B_HEADS)).reshape(1, LANES)
    lamv = jnp.stack([lambda_q1[0], lambda_k1[0], lambda_q2[0], lambda_k2[0]]).astype(F32)
    wo_a, wo_b = bf(w_out[0, :SEG]), bf(w_out[0, SEG:])
    f1 = (g_ffn1[0], bf(ffn1_w1[0]), bf(ffn1_w3[0]), bf(ffn1_w2[0]), g_mix[0])
    f2 = (g_ffn2[0], bf(ffn2_w1[0]), bf(ffn2_w3[0]), bf(ffn2_w2[0]), g_final[0])
    gains = (g_qa[0], g_ka[0], g_qb[0], g_kb[0])

    xp = x_prompt.reshape(bsz * seq, d)
    n_pad = -(n_small + N_META) % SMALL_ROW_MULT
    xs = jnp.concatenate([x_sample.reshape(n_small, d), meta_tokens.astype(x_sample.dtype),
                          jnp.zeros((n_pad, d), x_sample.dtype)], axis=0)
    pos_p = N_META + jnp.arange(seq)
    pos_s = jnp.concatenate([jnp.tile(past + jnp.arange(dseq), dbsz), jnp.arange(N_META),
                             jnp.zeros((n_pad,), jnp.int32)])

    x1p, hp = _ffn(xp, *f1, mode="emit_norm")
    x1s, hs = _ffn(xs, *f1, mode="emit_norm")
    (qa_p, kaf_p, kab_p, vaf_p, vab_p, qb_p, kbf_p, kbb_p, vbf_p, vbb_p, logf_p) = _project(
        hp, w_main, w_f, b_fp, *gains, pos_p, rows_per_seq=seq, gap=N_META)
    (qa_s, kaf_s, kab_s, vaf_s, vab_s, qb_s, kbf_s, kbb_s, vbf_s, vbb_s, logf_s) = _project(
        hs, w_main, w_f, b_fp, *gains, pos_s, rows_per_seq=xs.shape[0])

    logf_meta = logf_s[n_small:n_small + N_META]
    c_real = _cumsum_lanes(logf_p.reshape(bsz, seq, B_HEADS).transpose(0, 2, 1).reshape(bsz * B_HEADS, seq))
    c_meta = _cumsum_lanes(logf_meta.T)
    seq_s = jnp.concatenate([cache_b_logf[0].astype(F32), logf_s[:n_small].reshape(dbsz, dseq, B_HEADS)], axis=1)
    c_small = _cumsum_lanes(seq_s.transpose(0, 2, 1).reshape(dbsz * B_HEADS, past + dseq))
    tq = _attn_tile(seq)
    negc_p = (-c_real).reshape(bsz, PROMPT_GROUPS, PROMPT_UNITS, seq // tq, tq)
    negc_meta = (c_meta[:, N_META - 1:] - c_meta).reshape(PROMPT_GROUPS, PROMPT_UNITS, N_META)
    tb = _pick_tile(past, SAMPLE_BLOCK, LANES)
    negc_s_old = (-c_small[:, :past]).reshape(dbsz, B_HEADS, past // tb, tb)
    negc_s_new = (-c_small[:, past:]).reshape(dbsz, B_HEADS, dseq)

    meta = slice(n_small, n_small + N_META)
    oa_p = _attn_prompt("a", qa_p, kab_p, vab_p, kab_s[meta], vab_s[meta], (lamv,), g_oa[0],
                        batch=bsz, seq=seq, lam_init=lam_init)
    ob_p = _attn_prompt("b", qb_p, kbb_p, vbb_p, kbb_s[meta], vbb_s[meta], (negc_p, negc_meta), g_ob[0],
                        batch=bsz, seq=seq)
    rows = lambda c: c[0].reshape(-1, HEAD_DIM)
    cva = cache_a_v[0].reshape(dbsz, past, A_HEADS, 2, HEAD_DIM).swapaxes(2, 3).reshape(-1, HEAD_DIM)
    oa_s, ob_s = _attn_sample(
        qa_s, kab_s, vab_s, rows(cache_a_k), cva, qb_s, kbb_s, vbb_s, rows(cache_b_k), rows(cache_b_v),
        negc_s_old, negc_s_new, lamv, g_oa[0], g_ob[0], batch=dbsz, t=dseq, past=past, lam_init=lam_init)

    y_p = _ffn(_merge(x1p, oa_p, ob_p, wo_a, wo_b), *f2, mode="final_norm")
    y_s = _ffn(_merge(x1s[:n_small], oa_s, ob_s, wo_a, wo_b), *f2, mode="final_norm")

    def unslot(x, lead, tail, slots=None):
        if slots == VA_SLOTS:
            x = x.reshape(lead + (2, A_HEADS, HEAD_DIM)).swapaxes(-3, -2)
        return x.reshape(lead + tail)

    small_f32 = (kaf_s, vaf_s, kbf_s, vbf_s)
    meta_rows = [x[n_small * ROW_SLOTS:(n_small + N_META) * ROW_SLOTS] for x in small_f32]
    full = _fill_gaps((kaf_p, vaf_p, kbf_p, vbf_p), meta_rows, n_seq=bsz)
    tails = ((A_HEADS, 2, HEAD_DIM), (A_HEADS, 2 * HEAD_DIM), (B_HEADS, HEAD_DIM), (B_HEADS, HEAD_DIM))
    slots = (None, VA_SLOTS, None, None)
    ak_p, av_p, bk_p, bv_p = [unslot(x, (1, bsz, N_META + seq), tl, sl) for x, tl, sl in zip(full, tails, slots)]
    ak_s, av_s, bk_s, bv_s = [unslot(x[:n_small * ROW_SLOTS], (1, dbsz, dseq), tl, sl)
                              for x, tl, sl in zip(small_f32, tails, slots)]
    lf_meta = jnp.broadcast_to(logf_meta[None], (bsz, N_META, B_HEADS))
    lf_p = jnp.concatenate([lf_meta, logf_p.reshape(bsz, seq, B_HEADS)], axis=1)[None]
    lf_s = logf_s[:n_small].reshape(1, dbsz, dseq, B_HEADS)
    return (y_p.reshape(bsz, seq, d), y_s.reshape(dbsz, dseq, d),
            ak_p, av_p, bk_p, bv_p, lf_p, ak_s, av_s, bk_s, bv_s, lf_s)
```

```python
import functools
import math

import jax
import jax.numpy as jnp
import numpy as np
from jax import lax
from jax.experimental import pallas as pl
from jax.experimental.pallas import tpu as pltpu

F32 = jnp.float32
BF16 = jnp.bfloat16

HEAD_DIM = 128
A_HEADS = 4
B_HEADS = 8
SEG = 1024
N_SEG = 6
PROMPT_UNITS = 4
PROMPT_GROUP_W = PROMPT_UNITS * HEAD_DIM
PROMPT_GROUPS = SEG // PROMPT_GROUP_W
ROW_SLOTS = SEG // HEAD_DIM
VA_SLOTS = [(c % 2) * A_HEADS + c // 2 for c in range(ROW_SLOTS)]
N_META = 16
CHUNK = 64
ROPE_DIM = HEAD_DIM // 4
ROPE_THETA = 500000.0
EPS = 1e-6
NEG = -1e30
LOG2E = math.log2(math.e)
LANES = 128
VMEM_LIMIT = 60 * 1024 * 1024
SAMPLE_BLOCK = 512
SMALL_ROW_MULT = 512


def _params(*semantics):
    return pltpu.CompilerParams(dimension_semantics=semantics, vmem_limit_bytes=VMEM_LIMIT)


def _pick_tile(n, target, mult):
    best = None
    for t in range(mult, min(n, target) + 1, mult):
        if n % t == 0:
            best = t
    assert best is not None, (n, target, mult)
    return best


def _rms(x, g):
    ms = jnp.mean(x * x, axis=-1, keepdims=True)
    return (x * lax.rsqrt(ms + EPS)) * g


def _dot(a, b):
    return jnp.dot(a, b, preferred_element_type=F32)


def _dot_nt(a, b):
    return lax.dot_general(a, b, (((1,), (1,)), ((), ())), preferred_element_type=F32)


def _ffn_body(x_ref, g_ref, w1_ref, w3_ref, w2_ref, gout_ref, *rest, mode):
    if mode == "emit_norm":
        out_ref, h_ref, xn_scr = rest
    else:
        out_ref, xn_scr = rest
    f = pl.program_id(1)

    @pl.when(f == 0)
    def _():
        xn_scr[...] = _rms(x_ref[...], g_ref[...]).astype(BF16)
        out_ref[...] = jnp.zeros_like(out_ref)

    xn = xn_scr[...]
    h1 = _dot(xn, w1_ref[...])
    h3 = _dot(xn, w3_ref[...])
    gate = (h1 * jax.nn.sigmoid(h1)) * h3
    out_ref[...] += _dot(gate.astype(BF16), w2_ref[...])

    @pl.when(f == pl.num_programs(1) - 1)
    def _():
        y = x_ref[...] + 0.5 * out_ref[...]
        if mode == "emit_norm":
            out_ref[...] = y
            h_ref[...] = _rms(y, gout_ref[...]).astype(BF16)
        else:
            out_ref[...] = _rms(y, gout_ref[...])


def _col_tiles(w, tile):
    d, f = w.shape
    return w.reshape(d, f // tile, tile).transpose(1, 0, 2)


def _ffn(x, g_in, w1, w3, w2, g_out, *, mode):
    t, d = x.shape
    f, tf = w2.shape[0], w1.shape[2]
    tm = _pick_tile(t, 512, 16)
    row = pl.BlockSpec((tm, d), lambda i, j: (i, 0))
    vec = pl.BlockSpec((1, d), lambda i, j: (0, 0))
    out_shape = [jax.ShapeDtypeStruct((t, d), F32)]
    out_specs = [row]
    if mode == "emit_norm":
        out_shape.append(jax.ShapeDtypeStruct((t, d), BF16))
        out_specs.append(row)
    col_tile = pl.BlockSpec((None, d, tf), lambda i, j: (j, 0, 0))
    res = pl.pallas_call(
        functools.partial(_ffn_body, mode=mode),
        grid=(t // tm, f // tf),
        in_specs=[row, vec, col_tile, col_tile,
                  pl.BlockSpec((tf, d), lambda i, j: (j, 0)),
                  vec],
        out_specs=out_specs,
        out_shape=out_shape,
        scratch_shapes=[pltpu.VMEM((tm, d), BF16)],
        compiler_params=_params("parallel", "arbitrary"),
        name="ffn_" + mode,
    )(x, g_in.reshape(1, d), w1, w3, w2, g_out.reshape(1, d))
    return res if mode == "emit_norm" else res[0]


def _proj_body(h_ref, w_ref, wf_ref, bf_ref, gqa_ref, gka_ref, gqb_ref, gkb_ref, tc_ref, ta_ref, tb_ref,
               qa_ref, kaf_ref, kab_ref, vaf_ref, vab_ref, qb_ref, kbf_ref, kbb_ref, vbf_ref, vbb_ref,
               logf_ref, acc_scr):
    j = pl.program_id(1)
    q_scale = HEAD_DIM ** -0.5 * LOG2E
    natural = list(range(ROW_SLOTS))
    segments = [(gqa_ref, True, None, None, qa_ref, q_scale),
                (gka_ref, True, kaf_ref, natural, kab_ref, None),
                (None, False, vaf_ref, VA_SLOTS, vab_ref, None),
                (gqb_ref, False, None, None, qb_ref, q_scale),
                (gkb_ref, False, kbf_ref, natural, kbb_ref, None),
                (None, False, vbf_ref, natural, vbb_ref, None)]

    def finish(k):
        g_ref, rope, f32_ref, slots, bf_ref_, scale = segments[k]
        for c in range(ROW_SLOTS):
            sl = slice(c * HEAD_DIM, (c + 1) * HEAD_DIM)
            y = acc_scr[k % 2, :, sl]
            if g_ref is not None:
                y = _rms(y, g_ref[...])
            if rope:
                y = (y * tc_ref[...] + pltpu.roll(y, HEAD_DIM - ROPE_DIM // 2, 1) * ta_ref[...]
                     + pltpu.roll(y, ROPE_DIM // 2, 1) * tb_ref[...])
            if f32_ref is not None:
                f32_ref[pl.ds(slots[c], y.shape[0], stride=ROW_SLOTS), :] = y
            bf_ref_[:, sl] = (y if scale is None else y * scale).astype(BF16)

    for k in range(N_SEG + 1):
        @pl.when(j == k)
        def _(k=k):
            if k < N_SEG:
                acc_scr[k % 2] = _dot(h_ref[...], w_ref[...])
            if k == 0:
                z = _dot(h_ref[...], wf_ref[...]) + bf_ref[...]
                logf = jnp.minimum(z, 0.0) - jnp.log1p(jnp.exp(-jnp.abs(z)))
                logf_ref[...] = logf[:, :B_HEADS]
            else:
                finish(k - 1)


def _rope_tables(pos):
    half = ROPE_DIM // 2
    inv = jnp.power(ROPE_THETA, -jnp.arange(half, dtype=F32) * 2.0 / ROPE_DIM)
    ang = pos.astype(F32)[:, None] * inv[None, :]
    cos, sin = jnp.cos(ang), jnp.sin(ang)
    n = pos.shape[0]
    zeros = jnp.zeros((n, HEAD_DIM - ROPE_DIM), F32)
    tab_c = jnp.concatenate([cos, cos, jnp.ones((n, HEAD_DIM - ROPE_DIM), F32)], axis=1)
    tab_a = jnp.concatenate([-sin, jnp.zeros((n, half), F32), zeros], axis=1)
    tab_b = jnp.concatenate([jnp.zeros((n, half), F32), sin, zeros], axis=1)
    return tab_c, tab_a, tab_b


def _project(h, w_main, w_f, b_f, g_qa, g_ka, g_qb, g_kb, pos, *, rows_per_seq, gap=0):
    t, d = h.shape
    tm = _pick_tile(rows_per_seq, 512, 16)
    tiles_per_seq = rows_per_seq // tm
    tabs = _rope_tables(pos)
    row = lambda w: pl.BlockSpec((tm, w), lambda i, j: (i, 0))
    wide = row(SEG)
    first_row = lambda i: ((i // tiles_per_seq) * (gap + rows_per_seq) + gap + (i % tiles_per_seq) * tm) * ROW_SLOTS
    tall = pl.BlockSpec((pl.Element(tm * ROW_SLOTS), pl.Element(HEAD_DIM)), lambda i, j: (first_row(i), 0))
    gain = pl.BlockSpec((1, HEAD_DIM), lambda i, j: (0, 0))
    tab = pl.BlockSpec((tm, HEAD_DIM), lambda i, j: (i % tiles_per_seq, 0))
    bf = jax.ShapeDtypeStruct((t, SEG), BF16)
    f32 = jax.ShapeDtypeStruct(((t // rows_per_seq) * (gap + rows_per_seq) * ROW_SLOTS, HEAD_DIM), F32)
    return pl.pallas_call(
        _proj_body,
        grid=(t // tm, N_SEG + 1),
        in_specs=[row(d),
                  pl.BlockSpec((None, d, SEG), lambda i, j: (jnp.minimum(j, N_SEG - 1), 0, 0)),
                  pl.BlockSpec((d, LANES), lambda i, j: (0, 0)),
                  pl.BlockSpec((1, LANES), lambda i, j: (0, 0)),
                  gain, gain, gain, gain, tab, tab, tab],
        out_specs=[wide, tall, wide, tall, wide, wide, tall, wide, tall, wide, row(B_HEADS)],
        out_shape=[bf, f32, bf, f32, bf, bf, f32, bf, f32, bf, jax.ShapeDtypeStruct((t, B_HEADS), F32)],
        scratch_shapes=[pltpu.VMEM((2, tm, SEG), F32)],
        compiler_params=_params("parallel", "arbitrary"),
        name="in_proj",
    )(h, w_main, w_f, b_f, g_qa.reshape(1, -1), g_ka.reshape(1, -1), g_qb.reshape(1, -1),
      g_kb.reshape(1, -1), *tabs)


def _fill_body(*refs):
    n = len(refs) // 3
    for src_ref, out_ref in zip(refs[n:2 * n], refs[2 * n:]):
        out_ref[...] = src_ref[...]


def _fill_gaps(dsts, srcs, *, n_seq):
    gap_rows = srcs[0].shape[0]
    blocks_per_seq = dsts[0].shape[0] // n_seq // gap_rows
    assert blocks_per_seq * gap_rows * n_seq == dsts[0].shape[0]
    n = len(dsts)
    return pl.pallas_call(
        _fill_body,
        grid=(n_seq,),
        in_specs=[pl.BlockSpec(memory_space=pl.ANY)] * n + [pl.BlockSpec(srcs[0].shape, lambda b: (0, 0))] * n,
        out_specs=[pl.BlockSpec((gap_rows, HEAD_DIM), lambda b: (b * blocks_per_seq, 0))] * n,
        out_shape=[jax.ShapeDtypeStruct(x.shape, x.dtype) for x in dsts],
        input_output_aliases={k: k for k in range(n)},
        compiler_params=_params("parallel"),
        name="fill_meta_rows",
    )(*dsts, *srcs)


def _cumsum_body(x_ref, upper_ref, earlier_ref, o_ref):
    sb, nb, _ = x_ref.shape
    hi = lax.Precision.HIGHEST
    x = x_ref[...].reshape(sb * nb, LANES)
    within = jnp.dot(x, upper_ref[...], precision=hi, preferred_element_type=F32)
    totals = jnp.broadcast_to(within[:, LANES - 1:LANES], within.shape)
    carry = jnp.dot(earlier_ref[...], totals, precision=hi, preferred_element_type=F32)
    o_ref[...] = (within + carry).reshape(sb, nb, LANES)


def _cumsum_lanes(x):
    s, length = x.shape
    lp = -(-length // (8 * LANES)) * (8 * LANES)
    nb = lp // LANES
    sb = _pick_tile(s, max(1, 512 // nb), 1)
    r = sb * nb
    xp = jnp.pad(x, ((0, 0), (0, lp - length))).reshape(s, nb, LANES)
    idx = np.arange(LANES)
    upper = jnp.asarray(idx[:, None] <= idx[None, :], F32)
    rid = np.arange(r)
    earlier = jnp.asarray((rid[None, :] < rid[:, None]) & (rid[None, :] // nb == rid[:, None] // nb), F32)
    blk = pl.BlockSpec((sb, nb, LANES), lambda i: (i, 0, 0))
    out = pl.pallas_call(
        _cumsum_body,
        grid=(s // sb,),
        in_specs=[blk, pl.BlockSpec((LANES, LANES), lambda i: (0, 0)), pl.BlockSpec((r, r), lambda i: (0, 0))],
        out_specs=blk,
        out_shape=jax.ShapeDtypeStruct((s, nb, LANES), F32),
        compiler_params=_params("parallel"),
        name="cumsum",
    )(xp, upper, earlier)
    return out.reshape(s, lp)[:, :length]


def _lam(lamv_ref, lam_init):
    v = lamv_ref[...]
    s1 = jnp.sum(v[0:1] * v[1:2], axis=-1, keepdims=True)
    s2 = jnp.sum(v[2:3] * v[3:4], axis=-1, keepdims=True)
    return jnp.exp(s1) - jnp.exp(s2) + lam_init


def _lanes(col):
    return jnp.broadcast_to(col, (col.shape[0], LANES))


def _wide(stat, width):
    return jnp.tile(stat, (1, width // LANES))


def _with_ones(v):
    return jnp.concatenate([v, jnp.ones((v.shape[0], LANES), v.dtype)], axis=1)


def _advance_max(ss, m_ref):
    m_prev = m_ref[...]
    m_new = m_prev
    for s in ss:
        m_new = jnp.maximum(m_new, jnp.max(s, axis=1)[:, None])
    m_ref[...] = m_new
    return m_new, jnp.exp2(m_prev - m_new)


def _diag_mask(s, chunk):
    rows = lax.broadcasted_iota(jnp.int32, s.shape, 0) // chunk
    cols = lax.broadcasted_iota(jnp.int32, s.shape, 1) // chunk
    return jnp.where(cols <= rows, s, NEG)


def _probs_init(s, m_ref):
    m = jnp.max(s, axis=1)[:, None]
    m_ref[...] = _lanes(m)
    return jnp.exp2(s - m)


def _attn_tile(seq):
    return _pick_tile(seq, 512, CHUNK)


def _key_tile(ref, kt, tk, cols):
    return ref[pl.ds(pl.multiple_of(kt * tk, tk), tk), cols]


def _pipelined_tiles(qi, step):
    def below_diagonal(i, carry):
        step(2 * i, False)
        step(2 * i + 1, False)
        return carry

    lax.fori_loop(0, qi // 2, below_diagonal, 0)

    @pl.when(qi % 2 == 1)
    def _():
        step(qi - 1, False)

    step(qi, True)


def _attn_a_body(q_ref, k_ref, v_ref, km_ref, vm_ref, lamv_ref, g_ref, o_ref, m_scr, l_scr, acc_scr, p_scr,
                 *, lam_init):
    qi = pl.program_id(2)
    tq = q_ref.shape[0]
    wide_v = 2 * HEAD_DIM
    cols = lambda u: slice(u * HEAD_DIM, (u + 1) * HEAD_DIM)
    vcols = lambda u: slice((u // 2) * wide_v, (u // 2 + 1) * wide_v)

    for u in range(PROMPT_UNITS):
        p = _probs_init(_dot_nt(q_ref[:, cols(u)], km_ref[:, cols(u)]), m_scr.at[u])
        l_scr[u] = _lanes(jnp.sum(p, axis=1)[:, None])
        acc_scr[u] = _dot(p.astype(BF16), vm_ref[:, vcols(u)])
        p_scr[u] = jnp.zeros(p_scr.shape[1:], BF16)

    def step(kt, diagonal):
        prev = jnp.maximum(kt - 1, 0)
        for u in range(PROMPT_UNITS):
            s = _dot_nt(q_ref[:, cols(u)], _key_tile(k_ref, kt, tq, cols(u)))
            if diagonal:
                s = _diag_mask(s, CHUNK)
            flushed = acc_scr[u] + _dot(p_scr[u], _key_tile(v_ref, prev, tq, vcols(u)))
            m_new, alpha = _advance_max([s], m_scr.at[u])
            p = jnp.exp2(s - _wide(m_new, tq))
            l_scr[u] = alpha * l_scr[u] + _lanes(jnp.sum(p, axis=1)[:, None])
            p_scr[u] = p.astype(BF16)
            acc_scr[u] = _wide(alpha, wide_v) * flushed

    _pipelined_tiles(qi, step)
    lam = _lam(lamv_ref, lam_init)
    for h in range(PROMPT_UNITS // 2):
        o0, o1 = [(acc_scr[u] + _dot(p_scr[u], _key_tile(v_ref, qi, tq, vcols(u)))) / _wide(l_scr[u], wide_v)
                  for u in (2 * h, 2 * h + 1)]
        o_ref[:, vcols(2 * h)] = (_rms(o0 - lam * o1, g_ref[...]) * (1.0 - lam_init)).astype(BF16)


def _attn_b_body(q_ref, k_ref, v_ref, km_ref, vm_ref, nc_ref, ncm_ref, g_ref, o_ref, m_scr, acc_scr, p_scr):
    qi = pl.program_id(2)
    tq = q_ref.shape[0]
    cols = lambda u: slice(u * HEAD_DIM, (u + 1) * HEAD_DIM)

    for u in range(PROMPT_UNITS):
        s = _dot_nt(q_ref[:, cols(u)], km_ref[:, cols(u)]) + ncm_ref[u:u + 1, :] * LOG2E
        p = _probs_init(s, m_scr.at[u])
        acc_scr[u] = _dot(p.astype(BF16), _with_ones(vm_ref[:, cols(u)]))
        p_scr[u] = jnp.zeros(p_scr.shape[1:], BF16)

    def step(kt, diagonal):
        prev = jnp.maximum(kt - 1, 0)
        for u in range(PROMPT_UNITS):
            s = _dot_nt(q_ref[:, cols(u)], _key_tile(k_ref, kt, tq, cols(u))) + nc_ref[u, pl.ds(kt, 1), :] * LOG2E
            if diagonal:
                s = _diag_mask(s, 1)
            flushed = acc_scr[u] + _dot(p_scr[u], _with_ones(_key_tile(v_ref, prev, tq, cols(u))))
            m_new, alpha = _advance_max([s], m_scr.at[u])
            p_scr[u] = jnp.exp2(s - _wide(m_new, tq)).astype(BF16)
            acc_scr[u] = _wide(alpha, 2 * HEAD_DIM) * flushed

    _pipelined_tiles(qi, step)
    for u in range(PROMPT_UNITS):
        acc = acc_scr[u] + _dot(p_scr[u], _with_ones(_key_tile(v_ref, qi, tq, cols(u))))
        o_ref[:, cols(u)] = _rms(acc[:, :HEAD_DIM] / acc[:, HEAD_DIM:], g_ref[...]).astype(BF16)


def _attn_prompt(mode, q, k, v, km, vm, extra, g_o, *, batch, seq, lam_init=None):
    tq = _attn_tile(seq)
    nq = seq // tq
    q_spec = pl.BlockSpec((tq, PROMPT_GROUP_W), lambda b, h, i: (b * nq + i, h))
    kv_spec = pl.BlockSpec((seq, PROMPT_GROUP_W), lambda b, h, i: (b, h))
    meta_spec = pl.BlockSpec((N_META, PROMPT_GROUP_W), lambda b, h, i: (0, h))
    stat = pltpu.VMEM((PROMPT_UNITS, tq, LANES), F32)
    acc = pltpu.VMEM((PROMPT_UNITS, tq, 2 * HEAD_DIM), F32)
    pend = pltpu.VMEM((PROMPT_UNITS, tq, tq), BF16)
    if mode == "a":
        lamv, = extra
        body = functools.partial(_attn_a_body, lam_init=lam_init)
        extra_specs = [pl.BlockSpec(lamv.shape, lambda b, h, i: (0, 0))]
        scratch = [stat, stat, acc, pend]
    else:
        negc, negc_meta = extra
        body = _attn_b_body
        extra_specs = [pl.BlockSpec((None, None, PROMPT_UNITS, nq, tq), lambda b, h, i: (b, h, 0, 0, 0)),
                       pl.BlockSpec((None, PROMPT_UNITS, N_META), lambda b, h, i: (h, 0, 0))]
        scratch = [stat, acc, pend]
    g_spec = pl.BlockSpec((1, g_o.shape[-1]), lambda b, h, i: (0, 0))
    return pl.pallas_call(
        body,
        grid=(batch, PROMPT_GROUPS, nq),
        in_specs=[q_spec, kv_spec, kv_spec, meta_spec, meta_spec] + extra_specs + [g_spec],
        out_specs=q_spec,
        out_shape=jax.ShapeDtypeStruct((batch * seq, SEG), BF16),
        scratch_shapes=scratch,
        compiler_params=_params("parallel", "parallel", "arbitrary"),
        name="attn_prompt_" + mode,
    )(q, k, v, km, vm, *extra, g_o.reshape(1, -1))


def _slot_rows(ref, slot, n):
    return ref[pl.ds(slot, n, stride=ROW_SLOTS), :]


def _attn_sample_body(qa_ref, kan_ref, van_ref, cka_ref, cva_ref, qb_ref, kbn_ref, vbn_ref, ckb_ref, cvb_ref,
                      nco_ref, ncn_ref, lamv_ref, goa_ref, gob_ref, oa_ref, ob_ref,
                      m_scr, l_scr, acca_scr, accb_scr, *, lam_init):
    j = pl.program_id(1)
    t = qa_ref.shape[0]
    tb = cka_ref.shape[0] // ROW_SLOTS
    wide_v = 2 * HEAD_DIM
    cols = lambda u: slice(u * HEAD_DIM, (u + 1) * HEAD_DIM)
    n_a, n_b = 2 * A_HEADS, B_HEADS

    @pl.when(j == 0)
    def _():
        m_scr[...] = jnp.full(m_scr.shape, NEG, F32)
        l_scr[...] = jnp.zeros_like(l_scr)
        acca_scr[...] = jnp.zeros_like(acca_scr)
        accb_scr[...] = jnp.zeros_like(accb_scr)

    def update_a(u, s, v):
        m_new, alpha = _advance_max([s], m_scr.at[u])
        p = jnp.exp2(s - m_new[:, :1])
        l_scr[u] = alpha * l_scr[u] + _lanes(jnp.sum(p, axis=1)[:, None])
        acca_scr[u] = _wide(alpha, wide_v) * acca_scr[u] + _dot(p.astype(BF16), v)

    def update_b(h, s, v):
        m_new, alpha = _advance_max([s], m_scr.at[n_a + h])
        p = jnp.exp2(s - m_new[:, :1])
        accb_scr[h] = _wide(alpha, wide_v) * accb_scr[h] + _dot(p.astype(BF16), _with_ones(v))

    for h in range(A_HEADS):
        v = jnp.concatenate([_slot_rows(cva_ref, half * A_HEADS + h, tb) for half in range(2)], axis=1).astype(BF16)
        for m in range(2):
            u = 2 * h + m
            update_a(u, _dot_nt(qa_ref[:, cols(u)], _slot_rows(cka_ref, u, tb).astype(BF16)), v)
    for h in range(n_b):
        s = _dot_nt(qb_ref[:, cols(h)], _slot_rows(ckb_ref, h, tb).astype(BF16)) + nco_ref[h, pl.ds(j, 1), :] * LOG2E
        update_b(h, s, _slot_rows(cvb_ref, h, tb).astype(BF16))

    @pl.when(j == pl.num_programs(1) - 1)
    def _():
        lam = _lam(lamv_ref, lam_init)
        for h in range(A_HEADS):
            for m in range(2):
                u = 2 * h + m
                update_a(u, _dot_nt(qa_ref[:, cols(u)], kan_ref[:, cols(u)]), van_ref[:, h * wide_v:(h + 1) * wide_v])
            o0, o1 = [acca_scr[u] / _wide(l_scr[u], wide_v) for u in (2 * h, 2 * h + 1)]
            oa_ref[:, h * wide_v:(h + 1) * wide_v] = (_rms(o0 - lam * o1, goa_ref[...]) * (1.0 - lam_init)).astype(BF16)
        for h in range(n_b):
            s = _dot_nt(qb_ref[:, cols(h)], kbn_ref[:, cols(h)]) + ncn_ref[h:h + 1, :] * LOG2E
            update_b(h, _diag_mask(s, 1), vbn_ref[:, cols(h)])
            acc = accb_scr[h]
            ob_ref[:, cols(h)] = _rms(acc[:, :HEAD_DIM] / acc[:, HEAD_DIM:], gob_ref[...]).astype(BF16)


def _attn_sample(qa, kan, van, cka, cva, qb, kbn, vbn, ckb, cvb, negc_old, negc_new, lamv, g_oa, g_ob, *,
                 batch, t, past, lam_init):
    nblk, tb = negc_old.shape[-2:]
    new = pl.BlockSpec((t, SEG), lambda b, j: (b, 0))
    old = pl.BlockSpec((tb * ROW_SLOTS, HEAD_DIM), lambda b, j: (b * nblk + j, 0))
    whole = lambda a: pl.BlockSpec(a.shape, lambda b, j: (0,) * a.ndim)
    goa, gob = g_oa.reshape(1, -1), g_ob.reshape(1, -1)
    out = jax.ShapeDtypeStruct((batch * t, SEG), BF16)
    n_a = 2 * A_HEADS
    return pl.pallas_call(
        functools.partial(_attn_sample_body, lam_init=lam_init),
        grid=(batch, nblk),
        in_specs=[new, new, new, old, old, new, new, new, old, old,
                  pl.BlockSpec((None, B_HEADS, nblk, tb), lambda b, j: (b, 0, 0, 0)),
                  pl.BlockSpec((None, B_HEADS, t), lambda b, j: (b, 0, 0)),
                  whole(lamv), whole(goa), whole(gob)],
        out_specs=[new, new],
        out_shape=[out, out],
        scratch_shapes=[pltpu.VMEM((n_a + B_HEADS, t, LANES), F32), pltpu.VMEM((n_a, t, LANES), F32),
                        pltpu.VMEM((n_a, t, 2 * HEAD_DIM), F32), pltpu.VMEM((B_HEADS, t, 2 * HEAD_DIM), F32)],
        compiler_params=_params("parallel", "arbitrary"),
        name="attn_sample",
    )(qa, kan, van, cka, cva, qb, kbn, vbn, ckb, cvb, negc_old, negc_new, lamv, goa, gob)


def _merge_body(x_ref, oa_ref, ob_ref, wa_ref, wb_ref, o_ref):
    o_ref[...] = x_ref[...] + _dot(oa_ref[...], wa_ref[...]) + _dot(ob_ref[...], wb_ref[...])


def _merge(x, oa, ob, w_a, w_b):
    t, d = x.shape
    tm = _pick_tile(t, 512, 16)
    row = lambda w: pl.BlockSpec((tm, w), lambda i: (i, 0))
    wsp = pl.BlockSpec((SEG, d), lambda i: (0, 0))
    return pl.pallas_call(
        _merge_body,
        grid=(t // tm,),
        in_specs=[row(d), row(SEG), row(SEG), wsp, wsp],
        out_specs=row(d),
        out_shape=jax.ShapeDtypeStruct((t, d), F32),
        compiler_params=_params("parallel"),
        name="out_proj",
    )(x, oa, ob, w_a, w_b)


def kernel(x_prompt, x_sample, cache_a_k, cache_a_v, cache_b_k, cache_b_v, cache_b_logf, meta_tokens, g_ffn1, ffn1_w1, ffn1_w3, ffn1_w2, g_mix, w_in, b_f, g_qa, g_ka, g_qb, g_kb, lambda_q1, lambda_k1, lambda_q2, lambda_k2, g_oa, g_ob, w_out, g_ffn2, ffn2_w1, ffn2_w3, ffn2_w2, g_final):
    depth = w_in.shape[0]
    assert depth == 1, "meta rows skip attention, which is only valid for a single layer"
    bsz, seq, d = x_prompt.shape
    dbsz, dseq, _ = x_sample.shape
    past = cache_a_k.shape[2]
    n_small = dbsz * dseq
    lam_init = 0.8 - 0.6 * math.exp(-0.3 * 0)
    bf = lambda a: a.astype(BF16)

    w_main = _col_tiles(bf(w_in[0, :, :N_SEG * SEG]), SEG)
    w_f = bf(jnp.pad(w_in[0, :, N_SEG * SEG:], ((0, 0), (0, LANES - B_HEADS))))
    b_fp = jnp.pad(b_f[0], (0, LANES - B_HEADS)).reshape(1, LANES)
    lamv = jnp.stack([lambda_q1[0], lambda_k1[0], lambda_q2[0], lambda_k2[0]]).astype(F32)
    wo_a, wo_b = bf(w_out[0, :SEG]), bf(w_out[0, SEG:])
    tf = _pick_tile(ffn1_w1.shape[2], 512, LANES)
    ct = lambda w: _col_tiles(bf(w[0]), tf)
    f1 = (g_ffn1[0], ct(ffn1_w1), ct(ffn1_w3), bf(ffn1_w2[0]), g_mix[0])
    f2 = (g_ffn2[0], ct(ffn2_w1), ct(ffn2_w3), bf(ffn2_w2[0]), g_final[0])
    gains = (g_qa[0], g_ka[0], g_qb[0], g_kb[0])

    xp = x_prompt.reshape(bsz * seq, d)
    n_pad = -(n_small + N_META) % SMALL_ROW_MULT
    xs = jnp.concatenate([x_sample.reshape(n_small, d), meta_tokens.astype(x_sample.dtype),
                          jnp.zeros((n_pad, d), x_sample.dtype)], axis=0)
    pos_p = N_META + jnp.arange(seq)
    pos_s = jnp.concatenate([jnp.tile(past + jnp.arange(dseq), dbsz), jnp.arange(N_META),
                             jnp.zeros((n_pad,), jnp.int32)])

    x1p, hp = _ffn(xp, *f1, mode="emit_norm")
    x1s, hs = _ffn(xs, *f1, mode="emit_norm")
    (qa_p, kaf_p, kab_p, vaf_p, vab_p, qb_p, kbf_p, kbb_p, vbf_p, vbb_p, logf_p) = _project(
        hp, w_main, w_f, b_fp, *gains, pos_p, rows_per_seq=seq, gap=N_META)
    (qa_s, kaf_s, kab_s, vaf_s, vab_s, qb_s, kbf_s, kbb_s, vbf_s, vbb_s, logf_s) = _project(
        hs, w_main, w_f, b_fp, *gains, pos_s, rows_per_seq=xs.shape[0])

    logf_meta = logf_s[n_small:n_small + N_META]
    c_real = _cumsum_lanes(logf_p.reshape(bsz, seq, B_HEADS).transpose(0, 2, 1).reshape(bsz * B_HEADS, seq))
    c_meta = _cumsum_lanes(logf_meta.T)
    seq_s = jnp.concatenate([cache_b_logf[0].astype(F32), logf_s[:n_small].reshape(dbsz, dseq, B_HEADS)], axis=1)
    c_small = _cumsum_lanes(seq_s.transpose(0, 2, 1).reshape(dbsz * B_HEADS, past + dseq))
    tq = _attn_tile(seq)
    negc_p = (-c_real).reshape(bsz, PROMPT_GROUPS, PROMPT_UNITS, seq // tq, tq)
    negc_meta = (c_meta[:, N_META - 1:] - c_meta).reshape(PROMPT_GROUPS, PROMPT_UNITS, N_META)
    tb = _pick_tile(past, SAMPLE_BLOCK, LANES)
    negc_s_old = (-c_small[:, :past]).reshape(dbsz, B_HEADS, past // tb, tb)
    negc_s_new = (-c_small[:, past:]).reshape(dbsz, B_HEADS, dseq)

    meta = slice(n_small, n_small + N_META)
    oa_p = _attn_prompt("a", qa_p, kab_p, vab_p, kab_s[meta], vab_s[meta], (lamv,), g_oa[0],
                        batch=bsz, seq=seq, lam_init=lam_init)
    ob_p = _attn_prompt("b", qb_p, kbb_p, vbb_p, kbb_s[meta], vbb_s[meta], (negc_p, negc_meta), g_ob[0],
                        batch=bsz, seq=seq)
    rows = lambda c: c[0].reshape(-1, HEAD_DIM)
    cva = cache_a_v[0].reshape(dbsz, past, A_HEADS, 2, HEAD_DIM).swapaxes(2, 3).reshape(-1, HEAD_DIM)
    oa_s, ob_s = _attn_sample(
        qa_s, kab_s, vab_s, rows(cache_a_k), cva, qb_s, kbb_s, vbb_s, rows(cache_b_k), rows(cache_b_v),
        negc_s_old, negc_s_new, lamv, g_oa[0], g_ob[0], batch=dbsz, t=dseq, past=past, lam_init=lam_init)

    y_p = _ffn(_merge(x1p, oa_p, ob_p, wo_a, wo_b), *f2, mode="final_norm")
    y_s = _ffn(_merge(x1s[:n_small], oa_s, ob_s, wo_a, wo_b), *f2, mode="final_norm")

    def unslot(x, lead, tail, slots=None):
        if slots == VA_SLOTS:
            x = x.reshape(lead + (2, A_HEADS, HEAD_DIM)).swapaxes(-3, -2)
        return x.reshape(lead + tail)

    small_f32 = (kaf_s, vaf_s, kbf_s, vbf_s)
    meta_rows = [x[n_small * ROW_SLOTS:(n_small + N_META) * ROW_SLOTS] for x in small_f32]
    full = _fill_gaps((kaf_p, vaf_p, kbf_p, vbf_p), meta_rows, n_seq=bsz)
    tails = ((A_HEADS, 2, HEAD_DIM), (A_HEADS, 2 * HEAD_DIM), (B_HEADS, HEAD_DIM), (B_HEADS, HEAD_DIM))
    slots = (None, VA_SLOTS, None, None)
    ak_p, av_p, bk_p, bv_p = [unslot(x, (1, bsz, N_META + seq), tl, sl) for x, tl, sl in zip(full, tails, slots)]
    ak_s, av_s, bk_s, bv_s = [unslot(x[:n_small * ROW_SLOTS], (1, dbsz, dseq), tl, sl)
                              for x, tl, sl in zip(small_f32, tails, slots)]
    lf_meta = jnp.broadcast_to(logf_meta[None], (bsz, N_META, B_HEADS))
    lf_p = jnp.concatenate([lf_meta, logf_p.reshape(bsz, seq, B_HEADS)], axis=1)[None]
    lf_s = logf_s[:n_small].reshape(1, dbsz, dseq, B_HEADS)
    return (y_p.reshape(bsz, seq, d), y_s.reshape(dbsz, dseq, d),
            ak_p, av_p, bk_p, bv_p, lf_p, ak_s, av_s, bk_s, bv_s, lf_s)
```

```python
import functools
import math

import jax
import jax.numpy as jnp
import numpy as np
from jax import lax
from jax.experimental import pallas as pl
from jax.experimental.pallas import tpu as pltpu

F32 = jnp.float32
BF16 = jnp.bfloat16

HEAD_DIM = 128
A_HEADS = 4
B_HEADS = 8
SEG = 1024
N_SEG = 6
PROMPT_UNITS = 4
PROMPT_GROUP_W = PROMPT_UNITS * HEAD_DIM
PROMPT_GROUPS = SEG // PROMPT_GROUP_W
ROW_SLOTS = SEG // HEAD_DIM
VA_SLOTS = [(c % 2) * A_HEADS + c // 2 for c in range(ROW_SLOTS)]
N_META = 16
CHUNK = 64
ROPE_DIM = HEAD_DIM // 4
ROPE_THETA = 500000.0
EPS = 1e-6
NEG = -1e30
LOG2E = math.log2(math.e)
LANES = 128
VMEM_LIMIT = 63 * 1024 * 1024
SAMPLE_BLOCK = 512
SMALL_ROW_MULT = 512


def _params(*semantics):
    return pltpu.CompilerParams(dimension_semantics=semantics, vmem_limit_bytes=VMEM_LIMIT)


def _pick_tile(n, target, mult):
    best = None
    for t in range(mult, min(n, target) + 1, mult):
        if n % t == 0:
            best = t
    assert best is not None, (n, target, mult)
    return best


def _rms(x, g):
    ms = jnp.mean(x * x, axis=-1, keepdims=True)
    return (x * lax.rsqrt(ms + EPS)) * g


def _dot(a, b):
    return jnp.dot(a, b, preferred_element_type=F32)


def _dot_nt(a, b):
    return lax.dot_general(a, b, (((1,), (1,)), ((), ())), preferred_element_type=F32)


def _ffn_body(x_ref, g_ref, w1_ref, w3_ref, w2_ref, gout_ref, *rest, mode):
    if mode == "emit_norm":
        out_ref, h_ref, xn_scr = rest
    else:
        out_ref, xn_scr = rest
    f = pl.program_id(1)

    @pl.when(f == 0)
    def _():
        xn_scr[...] = _rms(x_ref[...], g_ref[...]).astype(BF16)
        out_ref[...] = jnp.zeros_like(out_ref)

    xn = xn_scr[...]
    h1 = _dot(xn, w1_ref[...])
    h3 = _dot(xn, w3_ref[...])
    gate = (h1 * jax.nn.sigmoid(h1)) * h3
    out_ref[...] += _dot(gate.astype(BF16), w2_ref[...])

    @pl.when(f == pl.num_programs(1) - 1)
    def _():
        y = x_ref[...] + 0.5 * out_ref[...]
        if mode == "emit_norm":
            out_ref[...] = y
            h_ref[...] = _rms(y, gout_ref[...]).astype(BF16)
        else:
            out_ref[...] = _rms(y, gout_ref[...])


def _ffn(x, g_in, w1, w3, w2, g_out, *, mode):
    t, d = x.shape
    f = w1.shape[1]
    tm = _pick_tile(t, 512 if mode == "emit_norm" else 1024, 16)
    tf = _pick_tile(f, 512, LANES)
    row = pl.BlockSpec((tm, d), lambda i, j: (i, 0))
    vec = pl.BlockSpec((1, d), lambda i, j: (0, 0))
    out_shape = [jax.ShapeDtypeStruct((t, d), F32)]
    out_specs = [row]
    if mode == "emit_norm":
        out_shape.append(jax.ShapeDtypeStruct((t, d), BF16))
        out_specs.append(row)
    res = pl.pallas_call(
        functools.partial(_ffn_body, mode=mode),
        grid=(t // tm, f // tf),
        in_specs=[row, vec,
                  pl.BlockSpec((d, tf), lambda i, j: (0, j)),
                  pl.BlockSpec((d, tf), lambda i, j: (0, j)),
                  pl.BlockSpec((tf, d), lambda i, j: (j, 0)),
                  vec],
        out_specs=out_specs,
        out_shape=out_shape,
        scratch_shapes=[pltpu.VMEM((tm, d), BF16)],
        compiler_params=_params("parallel", "arbitrary"),
        name="ffn_" + mode,
    )(x, g_in.reshape(1, d), w1, w3, w2, g_out.reshape(1, d))
    return res if mode == "emit_norm" else res[0]


_NATURAL = list(range(ROW_SLOTS))
_SEGMENTS = ((0, True, False, None, True),
             (1, True, True, _NATURAL, False),
             (None, False, True, VA_SLOTS, False),
             (2, False, False, None, True),
             (3, False, True, _NATURAL, False),
             (None, False, True, _NATURAL, False))


def _proj_body(*refs, segs, gate):
    h_ref, w_ref, wf_ref, bf_ref = refs[:4]
    gain_refs = refs[4:8]
    tc_ref, ta_ref, tb_ref = refs[8:11]
    outs, acc_scr = list(refs[11:-1]), refs[-1]
    seg_outs = [(outs.pop(0) if _SEGMENTS[s][2] else None, outs.pop(0)) for s in segs]
    j = pl.program_id(1)
    q_scale = HEAD_DIM ** -0.5 * LOG2E

    def finish(k):
        gain, rope, _, slots, is_query = _SEGMENTS[segs[k]]
        f32_ref, bf_ref_ = seg_outs[k]
        for c in range(ROW_SLOTS):
            sl = slice(c * HEAD_DIM, (c + 1) * HEAD_DIM)
            y = acc_scr[k % 2, :, sl]
            if gain is not None:
                y = _rms(y, gain_refs[gain][...])
            if rope:
                y = (y * tc_ref[...] + pltpu.roll(y, HEAD_DIM - ROPE_DIM // 2, 1) * ta_ref[...]
                     + pltpu.roll(y, ROPE_DIM // 2, 1) * tb_ref[...])
            if f32_ref is not None:
                f32_ref[pl.ds(slots[c], y.shape[0], stride=ROW_SLOTS), :] = y
            bf_ref_[:, sl] = (y * q_scale if is_query else y).astype(BF16)

    for k in range(len(segs) + 1):
        @pl.when(j == k)
        def _(k=k):
            if k < len(segs):
                acc_scr[k % 2] = _dot(h_ref[...], w_ref[...])
            if k == 0 and gate:
                z = _dot(h_ref[...], wf_ref[...]) + bf_ref[...]
                logf = jnp.minimum(z, 0.0) - jnp.log1p(jnp.exp(-jnp.abs(z)))
                outs[0][...] = logf[:, :B_HEADS]
            if k > 0:
                finish(k - 1)


def _rope_tables(pos):
    half = ROPE_DIM // 2
    inv = jnp.power(ROPE_THETA, -jnp.arange(half, dtype=F32) * 2.0 / ROPE_DIM)
    ang = pos.astype(F32)[:, None] * inv[None, :]
    cos, sin = jnp.cos(ang), jnp.sin(ang)
    n = pos.shape[0]
    zeros = jnp.zeros((n, HEAD_DIM - ROPE_DIM), F32)
    tab_c = jnp.concatenate([cos, cos, jnp.ones((n, HEAD_DIM - ROPE_DIM), F32)], axis=1)
    tab_a = jnp.concatenate([-sin, jnp.zeros((n, half), F32), zeros], axis=1)
    tab_b = jnp.concatenate([jnp.zeros((n, half), F32), sin, zeros], axis=1)
    return tab_c, tab_a, tab_b


def _project(h, w_main, w_f, b_f, g_qa, g_ka, g_qb, g_kb, pos, *, segs, gate, rows_per_seq, gap=0):
    t, d = h.shape
    tm = _pick_tile(rows_per_seq, 1024, 16)
    tiles_per_seq = rows_per_seq // tm
    n = len(segs)
    tabs = _rope_tables(pos)
    row = lambda w: pl.BlockSpec((tm, w), lambda i, j: (i, 0))
    first_row = lambda i: ((i // tiles_per_seq) * (gap + rows_per_seq) + gap + (i % tiles_per_seq) * tm) * ROW_SLOTS
    tall = pl.BlockSpec((pl.Element(tm * ROW_SLOTS), pl.Element(HEAD_DIM)), lambda i, j: (first_row(i), 0))
    gain = pl.BlockSpec((1, HEAD_DIM), lambda i, j: (0, 0))
    tab = pl.BlockSpec((tm, HEAD_DIM), lambda i, j: (i % tiles_per_seq, 0))
    bf = jax.ShapeDtypeStruct((t, SEG), BF16)
    f32 = jax.ShapeDtypeStruct(((t // rows_per_seq) * (gap + rows_per_seq) * ROW_SLOTS, HEAD_DIM), F32)
    out_specs, out_shape = [], []
    for s in segs:
        if _SEGMENTS[s][2]:
            out_specs.append(tall)
            out_shape.append(f32)
        out_specs.append(row(SEG))
        out_shape.append(bf)
    if gate:
        out_specs.append(row(B_HEADS))
        out_shape.append(jax.ShapeDtypeStruct((t, B_HEADS), F32))
    return pl.pallas_call(
        functools.partial(_proj_body, segs=tuple(segs), gate=gate),
        grid=(t // tm, n + 1),
        in_specs=[row(d),
                  pl.BlockSpec((d, SEG), lambda i, j: (0, segs[0] + jnp.minimum(j, n - 1))),
                  pl.BlockSpec((d, LANES), lambda i, j: (0, 0)),
                  pl.BlockSpec((1, LANES), lambda i, j: (0, 0)),
                  gain, gain, gain, gain, tab, tab, tab],
        out_specs=out_specs,
        out_shape=out_shape,
        scratch_shapes=[pltpu.VMEM((2, tm, SEG), F32)],
        compiler_params=_params("parallel", "arbitrary"),
        name="in_proj_" + "".join(str(s) for s in segs),
    )(h, w_main, w_f, b_f, g_qa.reshape(1, -1), g_ka.reshape(1, -1), g_qb.reshape(1, -1),
      g_kb.reshape(1, -1), *tabs)


def _fill_body(*refs):
    n = len(refs) // 3
    for src_ref, out_ref in zip(refs[n:2 * n], refs[2 * n:]):
        out_ref[...] = src_ref[...]


def _fill_gaps(dsts, srcs, *, n_seq):
    gap_rows = srcs[0].shape[0]
    blocks_per_seq = dsts[0].shape[0] // n_seq // gap_rows
    assert blocks_per_seq * gap_rows * n_seq == dsts[0].shape[0]
    n = len(dsts)
    return pl.pallas_call(
        _fill_body,
        grid=(n_seq,),
        in_specs=[pl.BlockSpec(memory_space=pl.ANY)] * n + [pl.BlockSpec(srcs[0].shape, lambda b: (0, 0))] * n,
        out_specs=[pl.BlockSpec((gap_rows, HEAD_DIM), lambda b: (b * blocks_per_seq, 0))] * n,
        out_shape=[jax.ShapeDtypeStruct(x.shape, x.dtype) for x in dsts],
        input_output_aliases={k: k for k in range(n)},
        compiler_params=_params("parallel"),
        name="fill_meta_rows",
    )(*dsts, *srcs)


def _cumsum_body(x_ref, upper_ref, earlier_ref, o_ref):
    sb, nb, _ = x_ref.shape
    hi = lax.Precision.HIGHEST
    x = x_ref[...].reshape(sb * nb, LANES)
    within = jnp.dot(x, upper_ref[...], precision=hi, preferred_element_type=F32)
    totals = jnp.broadcast_to(within[:, LANES - 1:LANES], within.shape)
    carry = jnp.dot(earlier_ref[...], totals, precision=hi, preferred_element_type=F32)
    o_ref[...] = (within + carry).reshape(sb, nb, LANES)


def _cumsum_lanes(x):
    s, length = x.shape
    lp = -(-length // (8 * LANES)) * (8 * LANES)
    nb = lp // LANES
    sb = _pick_tile(s, max(1, 512 // nb), 1)
    r = sb * nb
    xp = jnp.pad(x, ((0, 0), (0, lp - length))).reshape(s, nb, LANES)
    idx = np.arange(LANES)
    upper = jnp.asarray(idx[:, None] <= idx[None, :], F32)
    rid = np.arange(r)
    earlier = jnp.asarray((rid[None, :] < rid[:, None]) & (rid[None, :] // nb == rid[:, None] // nb), F32)
    blk = pl.BlockSpec((sb, nb, LANES), lambda i: (i, 0, 0))
    out = pl.pallas_call(
        _cumsum_body,
        grid=(s // sb,),
        in_specs=[blk, pl.BlockSpec((LANES, LANES), lambda i: (0, 0)), pl.BlockSpec((r, r), lambda i: (0, 0))],
        out_specs=blk,
        out_shape=jax.ShapeDtypeStruct((s, nb, LANES), F32),
        compiler_params=_params("parallel"),
        name="cumsum",
    )(xp, upper, earlier)
    return out.reshape(s, lp)[:, :length]


def _lam(lamv_ref, lam_init):
    v = lamv_ref[...]
    s1 = jnp.sum(v[0:1] * v[1:2], axis=-1, keepdims=True)
    s2 = jnp.sum(v[2:3] * v[3:4], axis=-1, keepdims=True)
    return jnp.exp(s1) - jnp.exp(s2) + lam_init


def _lanes(col):
    return jnp.broadcast_to(col, (col.shape[0], LANES))


def _wide(stat, width):
    return jnp.tile(stat, (1, width // LANES))


def _with_ones(v):
    return jnp.concatenate([v, jnp.ones((v.shape[0], LANES), v.dtype)], axis=1)


def _advance_max(ss, m_ref):
    m_prev = m_ref[...]
    m_new = m_prev
    for s in ss:
        m_new = jnp.maximum(m_new, jnp.max(s, axis=1)[:, None])
    m_ref[...] = m_new
    return m_new, jnp.exp2(m_prev - m_new)


def _diag_mask(s, chunk):
    rows = lax.broadcasted_iota(jnp.int32, s.shape, 0) // chunk
    cols = lax.broadcasted_iota(jnp.int32, s.shape, 1) // chunk
    return jnp.where(cols <= rows, s, NEG)


def _probs_init(s, m_ref):
    m = jnp.max(s, axis=1)[:, None]
    m_ref[...] = _lanes(m)
    return jnp.exp2(s - m)


def _attn_tile(seq):
    return _pick_tile(seq, 512, CHUNK)


def _key_tile(ref, kt, tk, cols):
    return ref[pl.ds(pl.multiple_of(kt * tk, tk), tk), cols]


def _pipelined_tiles(qi, step):
    def below_diagonal(i, carry):
        step(2 * i, False)
        step(2 * i + 1, False)
        return carry

    lax.fori_loop(0, qi // 2, below_diagonal, 0)

    @pl.when(qi % 2 == 1)
    def _():
        step(qi - 1, False)

    step(qi, True)


def _attn_a_body(q_ref, k_ref, v_ref, km_ref, vm_ref, lamv_ref, g_ref, o_ref, m_scr, l_scr, acc_scr, p_scr,
                 *, lam_init):
    qi = pl.program_id(2)
    tq = q_ref.shape[0]
    wide_v = 2 * HEAD_DIM
    cols = lambda u: slice(u * HEAD_DIM, (u + 1) * HEAD_DIM)
    vcols = lambda u: slice((u // 2) * wide_v, (u // 2 + 1) * wide_v)

    for u in range(PROMPT_UNITS):
        p = _probs_init(_dot_nt(q_ref[:, cols(u)], km_ref[:, cols(u)]), m_scr.at[u])
        l_scr[u] = _lanes(jnp.sum(p, axis=1)[:, None])
        acc_scr[u] = _dot(p.astype(BF16), vm_ref[:, vcols(u)])
        p_scr[u] = jnp.zeros(p_scr.shape[1:], BF16)

    def step(kt, diagonal):
        prev = jnp.maximum(kt - 1, 0)
        for u in range(PROMPT_UNITS):
            s = _dot_nt(q_ref[:, cols(u)], _key_tile(k_ref, kt, tq, cols(u)))
            if diagonal:
                s = _diag_mask(s, CHUNK)
            flushed = acc_scr[u] + _dot(p_scr[u], _key_tile(v_ref, prev, tq, vcols(u)))
            m_new, alpha = _advance_max([s], m_scr.at[u])
            p = jnp.exp2(s - _wide(m_new, tq))
            l_scr[u] = alpha * l_scr[u] + _lanes(jnp.sum(p, axis=1)[:, None])
            p_scr[u] = p.astype(BF16)
            acc_scr[u] = _wide(alpha, wide_v) * flushed

    _pipelined_tiles(qi, step)
    lam = _lam(lamv_ref, lam_init)
    for h in range(PROMPT_UNITS // 2):
        o0, o1 = [(acc_scr[u] + _dot(p_scr[u], _key_tile(v_ref, qi, tq, vcols(u)))) / _wide(l_scr[u], wide_v)
                  for u in (2 * h, 2 * h + 1)]
        o_ref[:, vcols(2 * h)] = (_rms(o0 - lam * o1, g_ref[...]) * (1.0 - lam_init)).astype(BF16)


def _attn_b_body(q_ref, k_ref, v_ref, km_ref, vm_ref, nc_ref, ncm_ref, g_ref, o_ref, m_scr, acc_scr, p_scr):
    qi = pl.program_id(2)
    tq = q_ref.shape[0]
    cols = lambda u: slice(u * HEAD_DIM, (u + 1) * HEAD_DIM)

    for u in range(PROMPT_UNITS):
        s = _dot_nt(q_ref[:, cols(u)], km_ref[:, cols(u)]) + ncm_ref[u:u + 1, :] * LOG2E
        p = _probs_init(s, m_scr.at[u])
        acc_scr[u] = _dot(p.astype(BF16), _with_ones(vm_ref[:, cols(u)]))
        p_scr[u] = jnp.zeros(p_scr.shape[1:], BF16)

    def step(kt, diagonal):
        prev = jnp.maximum(kt - 1, 0)
        for u in range(PROMPT_UNITS):
            s = _dot_nt(q_ref[:, cols(u)], _key_tile(k_ref, kt, tq, cols(u))) + nc_ref[u, pl.ds(kt, 1), :] * LOG2E
            if diagonal:
                s = _diag_mask(s, 1)
            flushed = acc_scr[u] + _dot(p_scr[u], _with_ones(_key_tile(v_ref, prev, tq, cols(u))))
            m_new, alpha = _advance_max([s], m_scr.at[u])
            p_scr[u] = jnp.exp2(s - _wide(m_new, tq)).astype(BF16)
            acc_scr[u] = _wide(alpha, 2 * HEAD_DIM) * flushed

    _pipelined_tiles(qi, step)
    for u in range(PROMPT_UNITS):
        acc = acc_scr[u] + _dot(p_scr[u], _with_ones(_key_tile(v_ref, qi, tq, cols(u))))
        o_ref[:, cols(u)] = _rms(acc[:, :HEAD_DIM] / acc[:, HEAD_DIM:], g_ref[...]).astype(BF16)


def _attn_prompt(mode, q, k, v, km, vm, extra, g_o, *, batch, seq, lam_init=None):
    tq = _attn_tile(seq)
    nq = seq // tq
    q_spec = pl.BlockSpec((tq, PROMPT_GROUP_W), lambda b, h, i: (b * nq + i, h))
    kv_spec = pl.BlockSpec((seq, PROMPT_GROUP_W), lambda b, h, i: (b, h))
    meta_spec = pl.BlockSpec((N_META, PROMPT_GROUP_W), lambda b, h, i: (0, h))
    stat = pltpu.VMEM((PROMPT_UNITS, tq, LANES), F32)
    acc = pltpu.VMEM((PROMPT_UNITS, tq, 2 * HEAD_DIM), F32)
    pend = pltpu.VMEM((PROMPT_UNITS, tq, tq), BF16)
    if mode == "a":
        lamv, = extra
        body = functools.partial(_attn_a_body, lam_init=lam_init)
        extra_specs = [pl.BlockSpec(lamv.shape, lambda b, h, i: (0, 0))]
        scratch = [stat, stat, acc, pend]
    else:
        negc, negc_meta = extra
        body = _attn_b_body
        extra_specs = [pl.BlockSpec((None, None, PROMPT_UNITS, nq, tq), lambda b, h, i: (b, h, 0, 0, 0)),
                       pl.BlockSpec((None, PROMPT_UNITS, N_META), lambda b, h, i: (h, 0, 0))]
        scratch = [stat, acc, pend]
    g_spec = pl.BlockSpec((1, g_o.shape[-1]), lambda b, h, i: (0, 0))
    return pl.pallas_call(
        body,
        grid=(batch, PROMPT_GROUPS, nq),
        in_specs=[q_spec, kv_spec, kv_spec, meta_spec, meta_spec] + extra_specs + [g_spec],
        out_specs=q_spec,
        out_shape=jax.ShapeDtypeStruct((batch * seq, SEG), BF16),
        scratch_shapes=scratch,
        compiler_params=_params("parallel", "parallel", "arbitrary"),
        name="attn_prompt_" + mode,
    )(q, k, v, km, vm, *extra, g_o.reshape(1, -1))


def _slot_rows(ref, slot, n):
    return ref[pl.ds(slot, n, stride=ROW_SLOTS), :]


def _attn_sample_body(qa_ref, kan_ref, van_ref, cka_ref, cva_ref, qb_ref, kbn_ref, vbn_ref, ckb_ref, cvb_ref,
                      nco_ref, ncn_ref, lamv_ref, goa_ref, gob_ref, oa_ref, ob_ref,
                      m_scr, l_scr, acca_scr, accb_scr, *, lam_init):
    j = pl.program_id(1)
    t = qa_ref.shape[0]
    tb = cka_ref.shape[0] // ROW_SLOTS
    wide_v = 2 * HEAD_DIM
    cols = lambda u: slice(u * HEAD_DIM, (u + 1) * HEAD_DIM)
    n_a, n_b = 2 * A_HEADS, B_HEADS

    @pl.when(j == 0)
    def _():
        m_scr[...] = jnp.full(m_scr.shape, NEG, F32)
        l_scr[...] = jnp.zeros_like(l_scr)
        acca_scr[...] = jnp.zeros_like(acca_scr)
        accb_scr[...] = jnp.zeros_like(accb_scr)

    def update_a(u, s, v):
        m_new, alpha = _advance_max([s], m_scr.at[u])
        p = jnp.exp2(s - m_new[:, :1])
        l_scr[u] = alpha * l_scr[u] + _lanes(jnp.sum(p, axis=1)[:, None])
        acca_scr[u] = _wide(alpha, wide_v) * acca_scr[u] + _dot(p.astype(BF16), v)

    def update_b(h, s, v):
        m_new, alpha = _advance_max([s], m_scr.at[n_a + h])
        p = jnp.exp2(s - m_new[:, :1])
        accb_scr[h] = _wide(alpha, wide_v) * accb_scr[h] + _dot(p.astype(BF16), _with_ones(v))

    for h in range(A_HEADS):
        v = jnp.concatenate([_slot_rows(cva_ref, half * A_HEADS + h, tb) for half in range(2)], axis=1).astype(BF16)
        for m in range(2):
            u = 2 * h + m
            update_a(u, _dot_nt(qa_ref[:, cols(u)], _slot_rows(cka_ref, u, tb).astype(BF16)), v)
    for h in range(n_b):
        s = _dot_nt(qb_ref[:, cols(h)], _slot_rows(ckb_ref, h, tb).astype(BF16)) + nco_ref[h, pl.ds(j, 1), :] * LOG2E
        update_b(h, s, _slot_rows(cvb_ref, h, tb).astype(BF16))

    @pl.when(j == pl.num_programs(1) - 1)
    def _():
        lam = _lam(lamv_ref, lam_init)
        for h in range(A_HEADS):
            for m in range(2):
                u = 2 * h + m
                update_a(u, _dot_nt(qa_ref[:, cols(u)], kan_ref[:, cols(u)]), van_ref[:, h * wide_v:(h + 1) * wide_v])
            o0, o1 = [acca_scr[u] / _wide(l_scr[u], wide_v) for u in (2 * h, 2 * h + 1)]
            oa_ref[:, h * wide_v:(h + 1) * wide_v] = (_rms(o0 - lam * o1, goa_ref[...]) * (1.0 - lam_init)).astype(BF16)
        for h in range(n_b):
            s = _dot_nt(qb_ref[:, cols(h)], kbn_ref[:, cols(h)]) + ncn_ref[h:h + 1, :] * LOG2E
            update_b(h, _diag_mask(s, 1), vbn_ref[:, cols(h)])
            acc = accb_scr[h]
            ob_ref[:, cols(h)] = _rms(acc[:, :HEAD_DIM] / acc[:, HEAD_DIM:], gob_ref[...]).astype(BF16)


def _attn_sample(qa, kan, van, cka, cva, qb, kbn, vbn, ckb, cvb, negc_old, negc_new, lamv, g_oa, g_ob, *,
                 batch, t, past, lam_init):
    nblk, tb = negc_old.shape[-2:]
    new = pl.BlockSpec((t, SEG), lambda b, j: (b, 0))
    old = pl.BlockSpec((tb * ROW_SLOTS, HEAD_DIM), lambda b, j: (b * nblk + j, 0))
    whole = lambda a: pl.BlockSpec(a.shape, lambda b, j: (0,) * a.ndim)
    goa, gob = g_oa.reshape(1, -1), g_ob.reshape(1, -1)
    out = jax.ShapeDtypeStruct((batch * t, SEG), BF16)
    n_a = 2 * A_HEADS
    return pl.pallas_call(
        functools.partial(_attn_sample_body, lam_init=lam_init),
        grid=(batch, nblk),
        in_specs=[new, new, new, old, old, new, new, new, old, old,
                  pl.BlockSpec((None, B_HEADS, nblk, tb), lambda b, j: (b, 0, 0, 0)),
                  pl.BlockSpec((None, B_HEADS, t), lambda b, j: (b, 0, 0)),
                  whole(lamv), whole(goa), whole(gob)],
        out_specs=[new, new],
        out_shape=[out, out],
        scratch_shapes=[pltpu.VMEM((n_a + B_HEADS, t, LANES), F32), pltpu.VMEM((n_a, t, LANES), F32),
                        pltpu.VMEM((n_a, t, 2 * HEAD_DIM), F32), pltpu.VMEM((B_HEADS, t, 2 * HEAD_DIM), F32)],
        compiler_params=_params("parallel", "arbitrary"),
        name="attn_sample",
    )(qa, kan, van, cka, cva, qb, kbn, vbn, ckb, cvb, negc_old, negc_new, lamv, goa, gob)


def _merge_body(x_ref, oa_ref, ob_ref, wa_ref, wb_ref, o_ref):
    o_ref[...] = x_ref[...] + _dot(oa_ref[...], wa_ref[...]) + _dot(ob_ref[...], wb_ref[...])


def _merge(x, oa, ob, w_a, w_b):
    t, d = x.shape
    tm = _pick_tile(t, 512, 16)
    row = lambda w: pl.BlockSpec((tm, w), lambda i: (i, 0))
    wsp = pl.BlockSpec((SEG, d), lambda i: (0, 0))
    return pl.pallas_call(
        _merge_body,
        grid=(t // tm,),
        in_specs=[row(d), row(SEG), row(SEG), wsp, wsp],
        out_specs=row(d),
        out_shape=jax.ShapeDtypeStruct((t, d), F32),
        compiler_params=_params("parallel"),
        name="out_proj",
    )(x, oa, ob, w_a, w_b)


def kernel(x_prompt, x_sample, cache_a_k, cache_a_v, cache_b_k, cache_b_v, cache_b_logf, meta_tokens, g_ffn1, ffn1_w1, ffn1_w3, ffn1_w2, g_mix, w_in, b_f, g_qa, g_ka, g_qb, g_kb, lambda_q1, lambda_k1, lambda_q2, lambda_k2, g_oa, g_ob, w_out, g_ffn2, ffn2_w1, ffn2_w3, ffn2_w2, g_final):
    depth = w_in.shape[0]
    assert depth == 1, "meta rows skip attention, which is only valid for a single layer"
    bsz, seq, d = x_prompt.shape
    dbsz, dseq, _ = x_sample.shape
    past = cache_a_k.shape[2]
    n_small = dbsz * dseq
    lam_init = 0.8 - 0.6 * math.exp(-0.3 * 0)
    bf = lambda a: a.astype(BF16)

    w_main = bf(w_in[0, :, :N_SEG * SEG])
    w_f = bf(jnp.pad(w_in[0, :, N_SEG * SEG:], ((0, 0), (0, LANES - B_HEADS))))
    b_fp = jnp.pad(b_f[0], (0, LANES - B_HEADS)).reshape(1, LANES)
    lamv = jnp.stack([lambda_q1[0], lambda_k1[0], lambda_q2[0], lambda_k2[0]]).astype(F32)
    wo_a, wo_b = bf(w_out[0, :SEG]), bf(w_out[0, SEG:])
    f1 = (g_ffn1[0], bf(ffn1_w1[0]), bf(ffn1_w3[0]), bf(ffn1_w2[0]), g_mix[0])
    f2 = (g_ffn2[0], bf(ffn2_w1[0]), bf(ffn2_w3[0]), bf(ffn2_w2[0]), g_final[0])

    xp = x_prompt.reshape(bsz * seq, d)
    n_pad = -(n_small + N_META) % SMALL_ROW_MULT
    xs = jnp.concatenate([x_sample.reshape(n_small, d), meta_tokens.astype(x_sample.dtype),
                          jnp.zeros((n_pad, d), x_sample.dtype)], axis=0)
    pos_p = N_META + jnp.arange(seq)
    pos_s = jnp.concatenate([jnp.tile(past + jnp.arange(dseq), dbsz), jnp.arange(N_META),
                             jnp.zeros((n_pad,), jnp.int32)])

    x1p, hp = _ffn(xp, *f1, mode="emit_norm")
    x1s, hs = _ffn(xs, *f1, mode="emit_norm")
    proj = functools.partial(_project, w_main=w_main, w_f=w_f, b_f=b_fp, g_qa=g_qa[0], g_ka=g_ka[0],
                             g_qb=g_qb[0], g_kb=g_kb[0])
    diff_segs, fox_segs = (0, 1, 2), (3, 4, 5)
    qa_p, kaf_p, kab_p, vaf_p, vab_p, logf_p = proj(hp, pos=pos_p, segs=diff_segs, gate=True, rows_per_seq=seq, gap=N_META)
    qb_p, kbf_p, kbb_p, vbf_p, vbb_p = proj(hp, pos=pos_p, segs=fox_segs, gate=False, rows_per_seq=seq, gap=N_META)
    qa_s, kaf_s, kab_s, vaf_s, vab_s, logf_s = proj(hs, pos=pos_s, segs=diff_segs, gate=True, rows_per_seq=xs.shape[0])
    qb_s, kbf_s, kbb_s, vbf_s, vbb_s = proj(hs, pos=pos_s, segs=fox_segs, gate=False, rows_per_seq=xs.shape[0])

    logf_meta = logf_s[n_small:n_small + N_META]
    c_real = _cumsum_lanes(logf_p.reshape(bsz, seq, B_HEADS).transpose(0, 2, 1).reshape(bsz * B_HEADS, seq))
    c_meta = _cumsum_lanes(logf_meta.T)
    seq_s = jnp.concatenate([cache_b_logf[0].astype(F32), logf_s[:n_small].reshape(dbsz, dseq, B_HEADS)], axis=1)
    c_small = _cumsum_lanes(seq_s.transpose(0, 2, 1).reshape(dbsz * B_HEADS, past + dseq))
    tq = _attn_tile(seq)
    negc_p = (-c_real).reshape(bsz, PROMPT_GROUPS, PROMPT_UNITS, seq // tq, tq)
    negc_meta = (c_meta[:, N_META - 1:] - c_meta).reshape(PROMPT_GROUPS, PROMPT_UNITS, N_META)
    tb = _pick_tile(past, SAMPLE_BLOCK, LANES)
    negc_s_old = (-c_small[:, :past]).reshape(dbsz, B_HEADS, past // tb, tb)
    negc_s_new = (-c_small[:, past:]).reshape(dbsz, B_HEADS, dseq)

    meta = slice(n_small, n_small + N_META)
    oa_p = _attn_prompt("a", qa_p, kab_p, vab_p, kab_s[meta], vab_s[meta], (lamv,), g_oa[0],
                        batch=bsz, seq=seq, lam_init=lam_init)
    ob_p = _attn_prompt("b", qb_p, kbb_p, vbb_p, kbb_s[meta], vbb_s[meta], (negc_p, negc_meta), g_ob[0],
                        batch=bsz, seq=seq)
    rows = lambda c: c[0].reshape(-1, HEAD_DIM)
    cva = cache_a_v[0].reshape(dbsz, past, A_HEADS, 2, HEAD_DIM).swapaxes(2, 3).reshape(-1, HEAD_DIM)
    oa_s, ob_s = _attn_sample(
        qa_s, kab_s, vab_s, rows(cache_a_k), cva, qb_s, kbb_s, vbb_s, rows(cache_b_k), rows(cache_b_v),
        negc_s_old, negc_s_new, lamv, g_oa[0], g_ob[0], batch=dbsz, t=dseq, past=past, lam_init=lam_init)

    y_p = _ffn(_merge(x1p, oa_p, ob_p, wo_a, wo_b), *f2, mode="final_norm")
    y_s = _ffn(_merge(x1s[:n_small], oa_s, ob_s, wo_a, wo_b), *f2, mode="final_norm")

    def unslot(x, lead, tail, slots=None):
        if slots == VA_SLOTS:
            x = x.reshape(lead + (2, A_HEADS, HEAD_DIM)).swapaxes(-3, -2)
        return x.reshape(lead + tail)

    small_f32 = (kaf_s, vaf_s, kbf_s, vbf_s)
    meta_rows = [x[n_small * ROW_SLOTS:(n_small + N_META) * ROW_SLOTS] for x in small_f32]
    full = _fill_gaps((kaf_p, vaf_p, kbf_p, vbf_p), meta_rows, n_seq=bsz)
    tails = ((A_HEADS, 2, HEAD_DIM), (A_HEADS, 2 * HEAD_DIM), (B_HEADS, HEAD_DIM), (B_HEADS, HEAD_DIM))
    slots = (None, VA_SLOTS, None, None)
    ak_p, av_p, bk_p, bv_p = [unslot(x, (1, bsz, N_META + seq), tl, sl) for x, tl, sl in zip(full, tails, slots)]
    ak_s, av_s, bk_s, bv_s = [unslot(x[:n_small * ROW_SLOTS], (1, dbsz, dseq), tl, sl)
                              for x, tl, sl in zip(small_f32, tails, slots)]
    lf_meta = jnp.broadcast_to(logf_meta[None], (bsz, N_META, B_HEADS))
    lf_p = jnp.concatenate([lf_meta, logf_p.reshape(bsz, seq, B_HEADS)], axis=1)[None]
    lf_s = logf_s[:n_small].reshape(1, dbsz, dseq, B_HEADS)
    return (y_p.reshape(bsz, seq, d), y_s.reshape(dbsz, dseq, d),
            ak_p, av_p, bk_p, bv_p, lf_p, ak_s, av_s, bk_s, bv_s, lf_s)
```

```python
import functools
import math

import jax
import jax.numpy as jnp
import numpy as np
from jax import lax
from jax.experimental import pallas as pl
from jax.experimental.pallas import tpu as pltpu

F32 = jnp.float32
BF16 = jnp.bfloat16

HEAD_DIM = 128
A_HEADS = 4
B_HEADS = 8
SEG = 1024
N_SEG = 6
PROMPT_UNITS = 4
PROMPT_GROUP_W = PROMPT_UNITS * HEAD_DIM
PROMPT_GROUPS = SEG // PROMPT_GROUP_W
ROW_SLOTS = SEG // HEAD_DIM
VA_SLOTS = [(c % 2) * A_HEADS + c // 2 for c in range(ROW_SLOTS)]
N_META = 16
CHUNK = 64
ROPE_DIM = HEAD_DIM // 4
ROPE_THETA = 500000.0
EPS = 1e-6
NEG = -1e30
LOG2E = math.log2(math.e)
LANES = 128
VMEM_LIMIT = 63 * 1024 * 1024
SAMPLE_BLOCK = 1024
SMALL_ROW_MULT = 512


def _params(*semantics):
    return pltpu.CompilerParams(dimension_semantics=semantics, vmem_limit_bytes=VMEM_LIMIT)


def _pick_tile(n, target, mult):
    best = None
    for t in range(mult, min(n, target) + 1, mult):
        if n % t == 0:
            best = t
    assert best is not None, (n, target, mult)
    return best


def _rms(x, g):
    ms = jnp.mean(x * x, axis=-1, keepdims=True)
    return (x * lax.rsqrt(ms + EPS)) * g


def _dot(a, b):
    return jnp.dot(a, b, preferred_element_type=F32)


def _dot_nt(a, b):
    return lax.dot_general(a, b, (((1,), (1,)), ((), ())), preferred_element_type=F32)


def _ffn_body(x_ref, g_ref, w1_ref, w3_ref, w2_ref, gout_ref, *rest, mode):
    if mode == "emit_norm":
        out_ref, h_ref, xn_scr = rest
    else:
        out_ref, xn_scr = rest
    f = pl.program_id(1)

    @pl.when(f == 0)
    def _():
        xn_scr[...] = _rms(x_ref[...], g_ref[...]).astype(BF16)
        out_ref[...] = jnp.zeros_like(out_ref)

    xn = xn_scr[...]
    h1 = _dot(xn, w1_ref[...])
    h3 = _dot(xn, w3_ref[...])
    gate = (h1 * jax.nn.sigmoid(h1)) * h3
    out_ref[...] += _dot(gate.astype(BF16), w2_ref[...])

    @pl.when(f == pl.num_programs(1) - 1)
    def _():
        y = x_ref[...] + 0.5 * out_ref[...]
        if mode == "emit_norm":
            out_ref[...] = y
            h_ref[...] = _rms(y, gout_ref[...]).astype(BF16)
        else:
            out_ref[...] = _rms(y, gout_ref[...])


def _ffn(x, g_in, w1, w3, w2, g_out, *, mode):
    t, d = x.shape
    f = w1.shape[1]
    tm = _pick_tile(t, 1024, 16)
    tf = _pick_tile(f, 256 if mode == "emit_norm" else 512, LANES)
    row = pl.BlockSpec((tm, d), lambda i, j: (i, 0))
    vec = pl.BlockSpec((1, d), lambda i, j: (0, 0))
    out_shape = [jax.ShapeDtypeStruct((t, d), F32)]
    out_specs = [row]
    if mode == "emit_norm":
        out_shape.append(jax.ShapeDtypeStruct((t, d), BF16))
        out_specs.append(row)
    res = pl.pallas_call(
        functools.partial(_ffn_body, mode=mode),
        grid=(t // tm, f // tf),
        in_specs=[row, vec,
                  pl.BlockSpec((d, tf), lambda i, j: (0, j)),
                  pl.BlockSpec((d, tf), lambda i, j: (0, j)),
                  pl.BlockSpec((tf, d), lambda i, j: (j, 0)),
                  vec],
        out_specs=out_specs,
        out_shape=out_shape,
        scratch_shapes=[pltpu.VMEM((tm, d), BF16)],
        compiler_params=_params("parallel", "arbitrary"),
        name="ffn_" + mode,
    )(x, g_in.reshape(1, d), w1, w3, w2, g_out.reshape(1, d))
    return res if mode == "emit_norm" else res[0]


_NATURAL = list(range(ROW_SLOTS))
_SEGMENTS = ((0, True, False, None, True),
             (1, True, True, _NATURAL, False),
             (None, False, True, VA_SLOTS, False),
             (2, False, False, None, True),
             (3, False, True, _NATURAL, False),
             (None, False, True, _NATURAL, False))


def _proj_body(*refs, segs, gate):
    h_ref, w_ref, wf_ref, bf_ref = refs[:4]
    gain_refs = refs[4:8]
    tc_ref, ta_ref, tb_ref = refs[8:11]
    outs, acc_scr = list(refs[11:-1]), refs[-1]
    seg_outs = [(outs.pop(0) if _SEGMENTS[s][2] else None, outs.pop(0)) for s in segs]
    j = pl.program_id(1)
    q_scale = HEAD_DIM ** -0.5 * LOG2E

    def finish(k):
        gain, rope, _, slots, is_query = _SEGMENTS[segs[k]]
        f32_ref, bf_ref_ = seg_outs[k]
        for c in range(ROW_SLOTS):
            sl = slice(c * HEAD_DIM, (c + 1) * HEAD_DIM)
            y = acc_scr[k % 2, :, sl]
            if gain is not None:
                y = _rms(y, gain_refs[gain][...])
            if rope:
                y = (y * tc_ref[...] + pltpu.roll(y, HEAD_DIM - ROPE_DIM // 2, 1) * ta_ref[...]
                     + pltpu.roll(y, ROPE_DIM // 2, 1) * tb_ref[...])
            if f32_ref is not None:
                f32_ref[pl.ds(slots[c], y.shape[0], stride=ROW_SLOTS), :] = y
            bf_ref_[:, sl] = (y * q_scale if is_query else y).astype(BF16)

    for k in range(len(segs)):
        @pl.when(j == k)
        def _(k=k):
            acc_scr[k % 2] = _dot(h_ref[...], w_ref[...])
            if k == 0 and gate:
                z = _dot(h_ref[...], wf_ref[...]) + bf_ref[...]
                logf = jnp.minimum(z, 0.0) - jnp.log1p(jnp.exp(-jnp.abs(z)))
                outs[0][...] = logf[:, :B_HEADS]
            if k > 0:
                finish(k - 1)
            if k == len(segs) - 1:
                finish(k)


def _rope_tables(pos):
    half = ROPE_DIM // 2
    inv = jnp.power(ROPE_THETA, -jnp.arange(half, dtype=F32) * 2.0 / ROPE_DIM)
    ang = pos.astype(F32)[:, None] * inv[None, :]
    cos, sin = jnp.cos(ang), jnp.sin(ang)
    n = pos.shape[0]
    zeros = jnp.zeros((n, HEAD_DIM - ROPE_DIM), F32)
    tab_c = jnp.concatenate([cos, cos, jnp.ones((n, HEAD_DIM - ROPE_DIM), F32)], axis=1)
    tab_a = jnp.concatenate([-sin, jnp.zeros((n, half), F32), zeros], axis=1)
    tab_b = jnp.concatenate([jnp.zeros((n, half), F32), sin, zeros], axis=1)
    return tab_c, tab_a, tab_b


def _project(h, w_main, w_f, b_f, g_qa, g_ka, g_qb, g_kb, pos, *, segs, gate, rows_per_seq, gap=0):
    t, d = h.shape
    tm = _pick_tile(rows_per_seq, 1024, 16)
    tiles_per_seq = rows_per_seq // tm
    n = len(segs)
    tabs = _rope_tables(pos)
    row = lambda w: pl.BlockSpec((tm, w), lambda i, j: (i, 0))
    first_row = lambda i: ((i // tiles_per_seq) * (gap + rows_per_seq) + gap + (i % tiles_per_seq) * tm) * ROW_SLOTS
    tall = pl.BlockSpec((pl.Element(tm * ROW_SLOTS), pl.Element(HEAD_DIM)), lambda i, j: (first_row(i), 0))
    gain = pl.BlockSpec((1, HEAD_DIM), lambda i, j: (0, 0))
    tab = pl.BlockSpec((tm, HEAD_DIM), lambda i, j: (i % tiles_per_seq, 0))
    bf = jax.ShapeDtypeStruct((t, SEG), BF16)
    f32 = jax.ShapeDtypeStruct(((t // rows_per_seq) * (gap + rows_per_seq) * ROW_SLOTS, HEAD_DIM), F32)
    out_specs, out_shape = [], []
    for s in segs:
        if _SEGMENTS[s][2]:
            out_specs.append(tall)
            out_shape.append(f32)
        out_specs.append(row(SEG))
        out_shape.append(bf)
    if gate:
        out_specs.append(row(B_HEADS))
        out_shape.append(jax.ShapeDtypeStruct((t, B_HEADS), F32))
    return pl.pallas_call(
        functools.partial(_proj_body, segs=tuple(segs), gate=gate),
        grid=(t // tm, n),
        in_specs=[row(d),
                  pl.BlockSpec((d, SEG), lambda i, j: (0, segs[0] + j)),
                  pl.BlockSpec((d, LANES), lambda i, j: (0, 0)),
                  pl.BlockSpec((1, LANES), lambda i, j: (0, 0)),
                  gain, gain, gain, gain, tab, tab, tab],
        out_specs=out_specs,
        out_shape=out_shape,
        scratch_shapes=[pltpu.VMEM((2, tm, SEG), F32)],
        compiler_params=_params("parallel", "arbitrary"),
        name="in_proj_" + "".join(str(s) for s in segs),
    )(h, w_main, w_f, b_f, g_qa.reshape(1, -1), g_ka.reshape(1, -1), g_qb.reshape(1, -1),
      g_kb.reshape(1, -1), *tabs)


def _fill_body(*refs):
    n = len(refs) // 3
    for src_ref, out_ref in zip(refs[n:2 * n], refs[2 * n:]):
        out_ref[...] = src_ref[...]


def _fill_gaps(dsts, srcs, *, n_seq):
    gap_rows = srcs[0].shape[0]
    blocks_per_seq = dsts[0].shape[0] // n_seq // gap_rows
    assert blocks_per_seq * gap_rows * n_seq == dsts[0].shape[0]
    n = len(dsts)
    return pl.pallas_call(
        _fill_body,
        grid=(n_seq,),
        in_specs=[pl.BlockSpec(memory_space=pl.ANY)] * n + [pl.BlockSpec(srcs[0].shape, lambda b: (0, 0))] * n,
        out_specs=[pl.BlockSpec((gap_rows, HEAD_DIM), lambda b: (b * blocks_per_seq, 0))] * n,
        out_shape=[jax.ShapeDtypeStruct(x.shape, x.dtype) for x in dsts],
        input_output_aliases={k: k for k in range(n)},
        compiler_params=_params("parallel"),
        name="fill_meta_rows",
    )(*dsts, *srcs)


def _cumsum_body(x_ref, upper_ref, earlier_ref, o_ref):
    sb, nb, _ = x_ref.shape
    hi = lax.Precision.HIGHEST
    x = x_ref[...].reshape(sb * nb, LANES)
    within = jnp.dot(x, upper_ref[...], precision=hi, preferred_element_type=F32)
    totals = jnp.broadcast_to(within[:, LANES - 1:LANES], within.shape)
    carry = jnp.dot(earlier_ref[...], totals, precision=hi, preferred_element_type=F32)
    o_ref[...] = (within + carry).reshape(sb, nb, LANES)


def _cumsum_lanes(x):
    s, length = x.shape
    lp = -(-length // (8 * LANES)) * (8 * LANES)
    nb = lp // LANES
    sb = _pick_tile(s, max(1, 512 // nb), 1)
    r = sb * nb
    xp = jnp.pad(x, ((0, 0), (0, lp - length))).reshape(s, nb, LANES)
    idx = np.arange(LANES)
    upper = jnp.asarray(idx[:, None] <= idx[None, :], F32)
    rid = np.arange(r)
    earlier = jnp.asarray((rid[None, :] < rid[:, None]) & (rid[None, :] // nb == rid[:, None] // nb), F32)
    blk = pl.BlockSpec((sb, nb, LANES), lambda i: (i, 0, 0))
    out = pl.pallas_call(
        _cumsum_body,
        grid=(s // sb,),
        in_specs=[blk, pl.BlockSpec((LANES, LANES), lambda i: (0, 0)), pl.BlockSpec((r, r), lambda i: (0, 0))],
        out_specs=blk,
        out_shape=jax.ShapeDtypeStruct((s, nb, LANES), F32),
        compiler_params=_params("parallel"),
        name="cumsum",
    )(xp, upper, earlier)
    return out.reshape(s, lp)[:, :length]


def _lam(lamv_ref, lam_init):
    v = lamv_ref[...]
    s1 = jnp.sum(v[0:1] * v[1:2], axis=-1, keepdims=True)
    s2 = jnp.sum(v[2:3] * v[3:4], axis=-1, keepdims=True)
    return jnp.exp(s1) - jnp.exp(s2) + lam_init


def _lanes(col):
    return jnp.broadcast_to(col, (col.shape[0], LANES))


def _wide(stat, width):
    return jnp.tile(stat, (1, width // LANES))


def _with_ones(v):
    return jnp.concatenate([v, jnp.ones((v.shape[0], LANES), v.dtype)], axis=1)


def _advance_max(ss, m_ref):
    m_prev = m_ref[...]
    m_new = m_prev
    for s in ss:
        m_new = jnp.maximum(m_new, jnp.max(s, axis=1)[:, None])
    m_ref[...] = m_new
    return m_new, jnp.exp2(m_prev - m_new)


def _diag_mask(s, chunk):
    rows = lax.broadcasted_iota(jnp.int32, s.shape, 0) // chunk
    cols = lax.broadcasted_iota(jnp.int32, s.shape, 1) // chunk
    return jnp.where(cols <= rows, s, NEG)


def _probs_init(s, m_ref):
    m = jnp.max(s, axis=1)[:, None]
    m_ref[...] = _lanes(m)
    return jnp.exp2(s - m)


def _attn_tile(seq):
    return _pick_tile(seq, 512, CHUNK)


def _key_tile(ref, kt, tk, cols):
    return ref[pl.ds(pl.multiple_of(kt * tk, tk), tk), cols]


def _pipelined_tiles(qi, step):
    def below_diagonal(i, carry):
        step(2 * i, False)
        step(2 * i + 1, False)
        return carry

    lax.fori_loop(0, qi // 2, below_diagonal, 0)

    @pl.when(qi % 2 == 1)
    def _():
        step(qi - 1, False)

    step(qi, True)


def _attn_a_body(q_ref, k_ref, v_ref, km_ref, vm_ref, lamv_ref, g_ref, o_ref, m_scr, l_scr, acc_scr, p_scr,
                 *, lam_init):
    qi = pl.program_id(2)
    tq = q_ref.shape[0]
    wide_v = 2 * HEAD_DIM
    cols = lambda u: slice(u * HEAD_DIM, (u + 1) * HEAD_DIM)
    vcols = lambda u: slice((u // 2) * wide_v, (u // 2 + 1) * wide_v)

    for u in range(PROMPT_UNITS):
        p = _probs_init(_dot_nt(q_ref[:, cols(u)], km_ref[:, cols(u)]), m_scr.at[u])
        l_scr[u] = _lanes(jnp.sum(p, axis=1)[:, None])
        acc_scr[u] = _dot(p.astype(BF16), vm_ref[:, vcols(u)])
        p_scr[u] = jnp.zeros(p_scr.shape[1:], BF16)

    def step(kt, diagonal):
        prev = jnp.maximum(kt - 1, 0)
        for u in range(PROMPT_UNITS):
            s = _dot_nt(q_ref[:, cols(u)], _key_tile(k_ref, kt, tq, cols(u)))
            if diagonal:
                s = _diag_mask(s, CHUNK)
            flushed = acc_scr[u] + _dot(p_scr[u], _key_tile(v_ref, prev, tq, vcols(u)))
            m_new, alpha = _advance_max([s], m_scr.at[u])
            p = jnp.exp2(s - _wide(m_new, tq))
            l_scr[u] = alpha * l_scr[u] + _lanes(jnp.sum(p, axis=1)[:, None])
            p_scr[u] = p.astype(BF16)
            acc_scr[u] = _wide(alpha, wide_v) * flushed

    _pipelined_tiles(qi, step)
    lam = _lam(lamv_ref, lam_init)
    for h in range(PROMPT_UNITS // 2):
        o0, o1 = [(acc_scr[u] + _dot(p_scr[u], _key_tile(v_ref, qi, tq, vcols(u)))) / _wide(l_scr[u], wide_v)
                  for u in (2 * h, 2 * h + 1)]
        o_ref[:, vcols(2 * h)] = (_rms(o0 - lam * o1, g_ref[...]) * (1.0 - lam_init)).astype(BF16)


def _attn_b_body(q_ref, k_ref, v_ref, km_ref, vm_ref, nc_ref, ncm_ref, g_ref, o_ref, m_scr, acc_scr, p_scr):
    qi = pl.program_id(2)
    tq = q_ref.shape[0]
    cols = lambda u: slice(u * HEAD_DIM, (u + 1) * HEAD_DIM)

    for u in range(PROMPT_UNITS):
        s = _dot_nt(q_ref[:, cols(u)], km_ref[:, cols(u)]) + ncm_ref[u:u + 1, :] * LOG2E
        p = _probs_init(s, m_scr.at[u])
        acc_scr[u] = _dot(p.astype(BF16), _with_ones(vm_ref[:, cols(u)]))
        p_scr[u] = jnp.zeros(p_scr.shape[1:], BF16)

    def step(kt, diagonal):
        prev = jnp.maximum(kt - 1, 0)
        for u in range(PROMPT_UNITS):
            s = _dot_nt(q_ref[:, cols(u)], _key_tile(k_ref, kt, tq, cols(u))) + nc_ref[u, pl.ds(kt, 1), :] * LOG2E
            if diagonal:
                s = _diag_mask(s, 1)
            flushed = acc_scr[u] + _dot(p_scr[u], _with_ones(_key_tile(v_ref, prev, tq, cols(u))))
            m_new, alpha = _advance_max([s], m_scr.at[u])
            p_scr[u] = jnp.exp2(s - _wide(m_new, tq)).astype(BF16)
            acc_scr[u] = _wide(alpha, 2 * HEAD_DIM) * flushed

    _pipelined_tiles(qi, step)
    for u in range(PROMPT_UNITS):
        acc = acc_scr[u] + _dot(p_scr[u], _with_ones(_key_tile(v_ref, qi, tq, cols(u))))
        o_ref[:, cols(u)] = _rms(acc[:, :HEAD_DIM] / acc[:, HEAD_DIM:], g_ref[...]).astype(BF16)


def _attn_prompt(mode, q, k, v, km, vm, extra, g_o, *, batch, seq, lam_init=None):
    tq = _attn_tile(seq)
    nq = seq // tq
    q_spec = pl.BlockSpec((tq, PROMPT_GROUP_W), lambda b, h, i: (b * nq + i, h))
    kv_spec = pl.BlockSpec((seq, PROMPT_GROUP_W), lambda b, h, i: (b, h))
    meta_spec = pl.BlockSpec((N_META, PROMPT_GROUP_W), lambda b, h, i: (0, h))
    stat = pltpu.VMEM((PROMPT_UNITS, tq, LANES), F32)
    acc = pltpu.VMEM((PROMPT_UNITS, tq, 2 * HEAD_DIM), F32)
    pend = pltpu.VMEM((PROMPT_UNITS, tq, tq), BF16)
    if mode == "a":
        lamv, = extra
        body = functools.partial(_attn_a_body, lam_init=lam_init)
        extra_specs = [pl.BlockSpec(lamv.shape, lambda b, h, i: (0, 0))]
        scratch = [stat, stat, acc, pend]
    else:
        negc, negc_meta = extra
        body = _attn_b_body
        extra_specs = [pl.BlockSpec((None, None, PROMPT_UNITS, nq, tq), lambda b, h, i: (b, h, 0, 0, 0)),
                       pl.BlockSpec((None, PROMPT_UNITS, N_META), lambda b, h, i: (h, 0, 0))]
        scratch = [stat, acc, pend]
    g_spec = pl.BlockSpec((1, g_o.shape[-1]), lambda b, h, i: (0, 0))
    return pl.pallas_call(
        body,
        grid=(batch, PROMPT_GROUPS, nq),
        in_specs=[q_spec, kv_spec, kv_spec, meta_spec, meta_spec] + extra_specs + [g_spec],
        out_specs=q_spec,
        out_shape=jax.ShapeDtypeStruct((batch * seq, SEG), BF16),
        scratch_shapes=scratch,
        compiler_params=_params("parallel", "parallel", "arbitrary"),
        name="attn_prompt_" + mode,
    )(q, k, v, km, vm, *extra, g_o.reshape(1, -1))


def _slot_rows(ref, slot, n):
    return ref[pl.ds(slot, n, stride=ROW_SLOTS), :]


def _attn_sample_body(qa_ref, kan_ref, van_ref, cka_ref, cva_ref, qb_ref, kbn_ref, vbn_ref, ckb_ref, cvb_ref,
                      nco_ref, ncn_ref, lamv_ref, goa_ref, gob_ref, oa_ref, ob_ref,
                      m_scr, l_scr, acca_scr, accb_scr, *, lam_init):
    j = pl.program_id(1)
    t = qa_ref.shape[0]
    tb = cka_ref.shape[0] // ROW_SLOTS
    wide_v = 2 * HEAD_DIM
    cols = lambda u: slice(u * HEAD_DIM, (u + 1) * HEAD_DIM)
    n_a, n_b = 2 * A_HEADS, B_HEADS

    @pl.when(j == 0)
    def _():
        m_scr[...] = jnp.full(m_scr.shape, NEG, F32)
        l_scr[...] = jnp.zeros_like(l_scr)
        acca_scr[...] = jnp.zeros_like(acca_scr)
        accb_scr[...] = jnp.zeros_like(accb_scr)

    def update_a(u, s, v):
        m_new, alpha = _advance_max([s], m_scr.at[u])
        p = jnp.exp2(s - m_new[:, :1])
        l_scr[u] = alpha * l_scr[u] + _lanes(jnp.sum(p, axis=1)[:, None])
        acca_scr[u] = _wide(alpha, wide_v) * acca_scr[u] + _dot(p.astype(BF16), v)

    def update_b(h, s, v):
        m_new, alpha = _advance_max([s], m_scr.at[n_a + h])
        p = jnp.exp2(s - m_new[:, :1])
        accb_scr[h] = _wide(alpha, wide_v) * accb_scr[h] + _dot(p.astype(BF16), _with_ones(v))

    for h in range(A_HEADS):
        v = jnp.concatenate([_slot_rows(cva_ref, half * A_HEADS + h, tb) for half in range(2)], axis=1).astype(BF16)
        for m in range(2):
            u = 2 * h + m
            update_a(u, _dot_nt(qa_ref[:, cols(u)], _slot_rows(cka_ref, u, tb).astype(BF16)), v)
    for h in range(n_b):
        s = _dot_nt(qb_ref[:, cols(h)], _slot_rows(ckb_ref, h, tb).astype(BF16)) + nco_ref[h, pl.ds(j, 1), :] * LOG2E
        update_b(h, s, _slot_rows(cvb_ref, h, tb).astype(BF16))

    @pl.when(j == pl.num_programs(1) - 1)
    def _():
        lam = _lam(lamv_ref, lam_init)
        for h in range(A_HEADS):
            for m in range(2):
                u = 2 * h + m
                update_a(u, _dot_nt(qa_ref[:, cols(u)], kan_ref[:, cols(u)]), van_ref[:, h * wide_v:(h + 1) * wide_v])
            o0, o1 = [acca_scr[u] / _wide(l_scr[u], wide_v) for u in (2 * h, 2 * h + 1)]
            oa_ref[:, h * wide_v:(h + 1) * wide_v] = (_rms(o0 - lam * o1, goa_ref[...]) * (1.0 - lam_init)).astype(BF16)
        for h in range(n_b):
            s = _dot_nt(qb_ref[:, cols(h)], kbn_ref[:, cols(h)]) + ncn_ref[h:h + 1, :] * LOG2E
            update_b(h, _diag_mask(s, 1), vbn_ref[:, cols(h)])
            acc = accb_scr[h]
            ob_ref[:, cols(h)] = _rms(acc[:, :HEAD_DIM] / acc[:, HEAD_DIM:], gob_ref[...]).astype(BF16)


def _attn_sample(qa, kan, van, cka, cva, qb, kbn, vbn, ckb, cvb, negc_old, negc_new, lamv, g_oa, g_ob, *,
                 batch, t, past, lam_init):
    nblk, tb = negc_old.shape[-2:]
    new = pl.BlockSpec((t, SEG), lambda b, j: (b, 0))
    old = pl.BlockSpec((tb * ROW_SLOTS, HEAD_DIM), lambda b, j: (b * nblk + j, 0))
    whole = lambda a: pl.BlockSpec(a.shape, lambda b, j: (0,) * a.ndim)
    goa, gob = g_oa.reshape(1, -1), g_ob.reshape(1, -1)
    out = jax.ShapeDtypeStruct((batch * t, SEG), BF16)
    n_a = 2 * A_HEADS
    return pl.pallas_call(
        functools.partial(_attn_sample_body, lam_init=lam_init),
        grid=(batch, nblk),
        in_specs=[new, new, new, old, old, new, new, new, old, old,
                  pl.BlockSpec((None, B_HEADS, nblk, tb), lambda b, j: (b, 0, 0, 0)),
                  pl.BlockSpec((None, B_HEADS, t), lambda b, j: (b, 0, 0)),
                  whole(lamv), whole(goa), whole(gob)],
        out_specs=[new, new],
        out_shape=[out, out],
        scratch_shapes=[pltpu.VMEM((n_a + B_HEADS, t, LANES), F32), pltpu.VMEM((n_a, t, LANES), F32),
                        pltpu.VMEM((n_a, t, 2 * HEAD_DIM), F32), pltpu.VMEM((B_HEADS, t, 2 * HEAD_DIM), F32)],
        compiler_params=_params("parallel", "arbitrary"),
        name="attn_sample",
    )(qa, kan, van, cka, cva, qb, kbn, vbn, ckb, cvb, negc_old, negc_new, lamv, goa, gob)


def _merge_body(x_ref, oa_ref, ob_ref, wa_ref, wb_ref, o_ref):
    o_ref[...] = x_ref[...] + _dot(oa_ref[...], wa_ref[...]) + _dot(ob_ref[...], wb_ref[...])


def _merge(x, oa, ob, w_a, w_b):
    t, d = x.shape
    tm = _pick_tile(t, 512, 16)
    row = lambda w: pl.BlockSpec((tm, w), lambda i: (i, 0))
    wsp = pl.BlockSpec((SEG, d), lambda i: (0, 0))
    return pl.pallas_call(
        _merge_body,
        grid=(t // tm,),
        in_specs=[row(d), row(SEG), row(SEG), wsp, wsp],
        out_specs=row(d),
        out_shape=jax.ShapeDtypeStruct((t, d), F32),
        compiler_params=_params("parallel"),
        name="out_proj",
    )(x, oa, ob, w_a, w_b)


def kernel(x_prompt, x_sample, cache_a_k, cache_a_v, cache_b_k, cache_b_v, cache_b_logf, meta_tokens, g_ffn1, ffn1_w1, ffn1_w3, ffn1_w2, g_mix, w_in, b_f, g_qa, g_ka, g_qb, g_kb, lambda_q1, lambda_k1, lambda_q2, lambda_k2, g_oa, g_ob, w_out, g_ffn2, ffn2_w1, ffn2_w3, ffn2_w2, g_final):
    depth = w_in.shape[0]
    assert depth == 1, "meta rows skip attention, which is only valid for a single layer"
    bsz, seq, d = x_prompt.shape
    dbsz, dseq, _ = x_sample.shape
    past = cache_a_k.shape[2]
    n_small = dbsz * dseq
    lam_init = 0.8 - 0.6 * math.exp(-0.3 * 0)
    bf = lambda a: a.astype(BF16)

    w_main = bf(w_in[0, :, :N_SEG * SEG])
    w_f = bf(jnp.pad(w_in[0, :, N_SEG * SEG:], ((0, 0), (0, LANES - B_HEADS))))
    b_fp = jnp.pad(b_f[0], (0, LANES - B_HEADS)).reshape(1, LANES)
    lamv = jnp.stack([lambda_q1[0], lambda_k1[0], lambda_q2[0], lambda_k2[0]]).astype(F32)
    wo_a, wo_b = bf(w_out[0, :SEG]), bf(w_out[0, SEG:])
    f1 = (g_ffn1[0], bf(ffn1_w1[0]), bf(ffn1_w3[0]), bf(ffn1_w2[0]), g_mix[0])
    f2 = (g_ffn2[0], bf(ffn2_w1[0]), bf(ffn2_w3[0]), bf(ffn2_w2[0]), g_final[0])

    xp = x_prompt.reshape(bsz * seq, d)
    n_pad = -(n_small + N_META) % SMALL_ROW_MULT
    xs = jnp.concatenate([x_sample.reshape(n_small, d), meta_tokens.astype(x_sample.dtype),
                          jnp.zeros((n_pad, d), x_sample.dtype)], axis=0)
    pos_p = N_META + jnp.arange(seq)
    pos_s = jnp.concatenate([jnp.tile(past + jnp.arange(dseq), dbsz), jnp.arange(N_META),
                             jnp.zeros((n_pad,), jnp.int32)])

    x1p, hp = _ffn(xp, *f1, mode="emit_norm")
    x1s, hs = _ffn(xs, *f1, mode="emit_norm")
    proj = functools.partial(_project, w_main=w_main, w_f=w_f, b_f=b_fp, g_qa=g_qa[0], g_ka=g_ka[0],
                             g_qb=g_qb[0], g_kb=g_kb[0])
    diff_segs, fox_segs = (0, 1, 2), (3, 4, 5)
    qa_p, kaf_p, kab_p, vaf_p, vab_p, logf_p = proj(hp, pos=pos_p, segs=diff_segs, gate=True, rows_per_seq=seq, gap=N_META)
    qb_p, kbf_p, kbb_p, vbf_p, vbb_p = proj(hp, pos=pos_p, segs=fox_segs, gate=False, rows_per_seq=seq, gap=N_META)
    qa_s, kaf_s, kab_s, vaf_s, vab_s, logf_s = proj(hs, pos=pos_s, segs=diff_segs, gate=True, rows_per_seq=xs.shape[0])
    qb_s, kbf_s, kbb_s, vbf_s, vbb_s = proj(hs, pos=pos_s, segs=fox_segs, gate=False, rows_per_seq=xs.shape[0])

    logf_meta = logf_s[n_small:n_small + N_META]
    c_real = _cumsum_lanes(logf_p.reshape(bsz, seq, B_HEADS).transpose(0, 2, 1).reshape(bsz * B_HEADS, seq))
    c_meta = _cumsum_lanes(logf_meta.T)
    seq_s = jnp.concatenate([cache_b_logf[0].astype(F32), logf_s[:n_small].reshape(dbsz, dseq, B_HEADS)], axis=1)
    c_small = _cumsum_lanes(seq_s.transpose(0, 2, 1).reshape(dbsz * B_HEADS, past + dseq))
    tq = _attn_tile(seq)
    negc_p = (-c_real).reshape(bsz, PROMPT_GROUPS, PROMPT_UNITS, seq // tq, tq)
    negc_meta = (c_meta[:, N_META - 1:] - c_meta).reshape(PROMPT_GROUPS, PROMPT_UNITS, N_META)
    tb = _pick_tile(past, SAMPLE_BLOCK, LANES)
    negc_s_old = (-c_small[:, :past]).reshape(dbsz, B_HEADS, past // tb, tb)
    negc_s_new = (-c_small[:, past:]).reshape(dbsz, B_HEADS, dseq)

    meta = slice(n_small, n_small + N_META)
    oa_p = _attn_prompt("a", qa_p, kab_p, vab_p, kab_s[meta], vab_s[meta], (lamv,), g_oa[0],
                        batch=bsz, seq=seq, lam_init=lam_init)
    ob_p = _attn_prompt("b", qb_p, kbb_p, vbb_p, kbb_s[meta], vbb_s[meta], (negc_p, negc_meta), g_ob[0],
                        batch=bsz, seq=seq)
    rows = lambda c: c[0].reshape(-1, HEAD_DIM)
    cva = cache_a_v[0].reshape(dbsz, past, A_HEADS, 2, HEAD_DIM).swapaxes(2, 3).reshape(-1, HEAD_DIM)
    oa_s, ob_s = _attn_sample(
        qa_s, kab_s, vab_s, rows(cache_a_k), cva, qb_s, kbb_s, vbb_s, rows(cache_b_k), rows(cache_b_v),
        negc_s_old, negc_s_new, lamv, g_oa[0], g_ob[0], batch=dbsz, t=dseq, past=past, lam_init=lam_init)

    y_p = _ffn(_merge(x1p, oa_p, ob_p, wo_a, wo_b), *f2, mode="final_norm")
    y_s = _ffn(_merge(x1s[:n_small], oa_s, ob_s, wo_a, wo_b), *f2, mode="final_norm")

    def unslot(x, lead, tail, slots=None):
        if slots == VA_SLOTS:
            x = x.reshape(lead + (2, A_HEADS, HEAD_DIM)).swapaxes(-3, -2)
        return x.reshape(lead + tail)

    small_f32 = (kaf_s, vaf_s, kbf_s, vbf_s)
    meta_rows = [x[n_small * ROW_SLOTS:(n_small + N_META) * ROW_SLOTS] for x in small_f32]
    full = _fill_gaps((kaf_p, vaf_p, kbf_p, vbf_p), meta_rows, n_seq=bsz)
    tails = ((A_HEADS, 2, HEAD_DIM), (A_HEADS, 2 * HEAD_DIM), (B_HEADS, HEAD_DIM), (B_HEADS, HEAD_DIM))
    slots = (None, VA_SLOTS, None, None)
    ak_p, av_p, bk_p, bv_p = [unslot(x, (1, bsz, N_META + seq), tl, sl) for x, tl, sl in zip(full, tails, slots)]
    ak_s, av_s, bk_s, bv_s = [unslot(x[:n_small * ROW_SLOTS], (1, dbsz, dseq), tl, sl)
                              for x, tl, sl in zip(small_f32, tails, slots)]
    lf_meta = jnp.broadcast_to(logf_meta[None], (bsz, N_META, B_HEADS))
    lf_p = jnp.concatenate([lf_meta, logf_p.reshape(bsz, seq, B_HEADS)], axis=1)[None]
    lf_s = logf_s[:n_small].reshape(1, dbsz, dseq, B_HEADS)
    return (y_p.reshape(bsz, seq, d), y_s.reshape(dbsz, dseq, d),
            ak_p, av_p, bk_p, bv_p, lf_p, ak_s, av_s, bk_s, bv_s, lf_s)
```

```python
import functools
import math

import jax
import jax.numpy as jnp
import numpy as np
from jax import lax
from jax.experimental import pallas as pl
from jax.experimental.pallas import tpu as pltpu

F32 = jnp.float32
BF16 = jnp.bfloat16

HEAD_DIM = 128
A_HEADS = 4
B_HEADS = 8
SEG = 1024
N_SEG = 6
PROMPT_UNITS = 4
PROMPT_GROUP_W = PROMPT_UNITS * HEAD_DIM
PROMPT_GROUPS = SEG // PROMPT_GROUP_W
ROW_SLOTS = SEG // HEAD_DIM
VA_SLOTS = [(c % 2) * A_HEADS + c // 2 for c in range(ROW_SLOTS)]
N_META = 16
CHUNK = 64
ROPE_DIM = HEAD_DIM // 4
ROPE_THETA = 500000.0
EPS = 1e-6
NEG = -1e30
LOG2E = math.log2(math.e)
LANES = 128
VMEM_LIMIT = 63 * 1024 * 1024
SAMPLE_BLOCK = 1024
SMALL_ROW_MULT = 16


def _params(*semantics):
    return pltpu.CompilerParams(dimension_semantics=semantics, vmem_limit_bytes=VMEM_LIMIT)


def _pick_tile(n, target, mult):
    best = None
    for t in range(mult, min(n, target) + 1, mult):
        if n % t == 0:
            best = t
    assert best is not None, (n, target, mult)
    return best


def _rms(x, g):
    ms = jnp.mean(x * x, axis=-1, keepdims=True)
    return (x * lax.rsqrt(ms + EPS)) * g


def _dot(a, b):
    return jnp.dot(a, b, preferred_element_type=F32)


def _dot_nt(a, b):
    return lax.dot_general(a, b, (((1,), (1,)), ((), ())), preferred_element_type=F32)


def _ffn_body(x_ref, g_ref, w1_ref, w3_ref, w2_ref, gout_ref, *rest, mode):
    if mode == "emit_norm":
        out_ref, h_ref, xn_scr = rest
    else:
        out_ref, xn_scr = rest
    f = pl.program_id(1)

    @pl.when(f == 0)
    def _():
        xn_scr[...] = _rms(x_ref[...], g_ref[...]).astype(BF16)
        out_ref[...] = jnp.zeros_like(out_ref)

    xn = xn_scr[...]
    h1 = _dot(xn, w1_ref[...])
    h3 = _dot(xn, w3_ref[...])
    gate = (h1 * jax.nn.sigmoid(h1)) * h3
    out_ref[...] += _dot(gate.astype(BF16), w2_ref[...])

    @pl.when(f == pl.num_programs(1) - 1)
    def _():
        y = x_ref[...] + 0.5 * out_ref[...]
        if mode == "emit_norm":
            out_ref[...] = y
            h_ref[...] = _rms(y, gout_ref[...]).astype(BF16)
        else:
            out_ref[...] = _rms(y, gout_ref[...])


def _ffn(x, g_in, w1, w3, w2, g_out, *, mode):
    t, d = x.shape
    f = w1.shape[1]
    tm = _pick_tile(t, 1024, 16)
    tf = _pick_tile(f, 256 if mode == "emit_norm" else 512, LANES)
    row = pl.BlockSpec((tm, d), lambda i, j: (i, 0))
    vec = pl.BlockSpec((1, d), lambda i, j: (0, 0))
    out_shape = [jax.ShapeDtypeStruct((t, d), F32)]
    out_specs = [row]
    if mode == "emit_norm":
        out_shape.append(jax.ShapeDtypeStruct((t, d), BF16))
        out_specs.append(row)
    res = pl.pallas_call(
        functools.partial(_ffn_body, mode=mode),
        grid=(t // tm, f // tf),
        in_specs=[row, vec,
                  pl.BlockSpec((d, tf), lambda i, j: (0, j)),
                  pl.BlockSpec((d, tf), lambda i, j: (0, j)),
                  pl.BlockSpec((tf, d), lambda i, j: (j, 0)),
                  vec],
        out_specs=out_specs,
        out_shape=out_shape,
        scratch_shapes=[pltpu.VMEM((tm, d), BF16)],
        compiler_params=_params("parallel", "arbitrary"),
        name="ffn_" + mode,
    )(x, g_in.reshape(1, d), w1, w3, w2, g_out.reshape(1, d))
    return res if mode == "emit_norm" else res[0]


_NATURAL = list(range(ROW_SLOTS))
_SEGMENTS = ((0, True, False, None, True),
             (1, True, True, _NATURAL, False),
             (None, False, True, VA_SLOTS, False),
             (2, False, False, None, True),
             (3, False, True, _NATURAL, False),
             (None, False, True, _NATURAL, False))


def _proj_body(*refs, segs, gate):
    h_ref, w_ref, wf_ref, bf_ref = refs[:4]
    gain_refs = refs[4:8]
    tc_ref, ta_ref, tb_ref = refs[8:11]
    outs, acc_scr = list(refs[11:-1]), refs[-1]
    seg_outs = [(outs.pop(0) if _SEGMENTS[s][2] else None, outs.pop(0)) for s in segs]
    j = pl.program_id(1)
    q_scale = HEAD_DIM ** -0.5 * LOG2E

    def finish(k):
        gain, rope, _, slots, is_query = _SEGMENTS[segs[k]]
        f32_ref, bf_ref_ = seg_outs[k]
        for c in range(ROW_SLOTS):
            sl = slice(c * HEAD_DIM, (c + 1) * HEAD_DIM)
            y = acc_scr[k % 2, :, sl]
            if gain is not None:
                y = _rms(y, gain_refs[gain][...])
            if rope:
                y = (y * tc_ref[...] + pltpu.roll(y, HEAD_DIM - ROPE_DIM // 2, 1) * ta_ref[...]
                     + pltpu.roll(y, ROPE_DIM // 2, 1) * tb_ref[...])
            if f32_ref is not None:
                f32_ref[pl.ds(slots[c], y.shape[0], stride=ROW_SLOTS), :] = y
            bf_ref_[:, sl] = (y * q_scale if is_query else y).astype(BF16)

    for k in range(len(segs)):
        @pl.when(j == k)
        def _(k=k):
            acc_scr[k % 2] = _dot(h_ref[...], w_ref[...])
            if k == 0 and gate:
                z = _dot(h_ref[...], wf_ref[...]) + bf_ref[...]
                logf = jnp.minimum(z, 0.0) - jnp.log1p(jnp.exp(-jnp.abs(z)))
                outs[0][...] = logf[:, :B_HEADS]
            if k > 0:
                finish(k - 1)
            if k == len(segs) - 1:
                finish(k)


def _rope_tables(pos):
    half = ROPE_DIM // 2
    inv = jnp.power(ROPE_THETA, -jnp.arange(half, dtype=F32) * 2.0 / ROPE_DIM)
    ang = pos.astype(F32)[:, None] * inv[None, :]
    cos, sin = jnp.cos(ang), jnp.sin(ang)
    n = pos.shape[0]
    zeros = jnp.zeros((n, HEAD_DIM - ROPE_DIM), F32)
    tab_c = jnp.concatenate([cos, cos, jnp.ones((n, HEAD_DIM - ROPE_DIM), F32)], axis=1)
    tab_a = jnp.concatenate([-sin, jnp.zeros((n, half), F32), zeros], axis=1)
    tab_b = jnp.concatenate([jnp.zeros((n, half), F32), sin, zeros], axis=1)
    return tab_c, tab_a, tab_b


def _project(h, w_main, w_f, b_f, g_qa, g_ka, g_qb, g_kb, pos, *, segs, gate, rows_per_seq, gap=0):
    t, d = h.shape
    tm = _pick_tile(rows_per_seq, 1024, 16)
    tiles_per_seq = rows_per_seq // tm
    n = len(segs)
    tabs = _rope_tables(pos)
    row = lambda w: pl.BlockSpec((tm, w), lambda i, j: (i, 0))
    first_row = lambda i: ((i // tiles_per_seq) * (gap + rows_per_seq) + gap + (i % tiles_per_seq) * tm) * ROW_SLOTS
    tall = pl.BlockSpec((pl.Element(tm * ROW_SLOTS), pl.Element(HEAD_DIM)), lambda i, j: (first_row(i), 0))
    gain = pl.BlockSpec((1, HEAD_DIM), lambda i, j: (0, 0))
    tab = pl.BlockSpec((tm, HEAD_DIM), lambda i, j: (i % tiles_per_seq, 0))
    bf = jax.ShapeDtypeStruct((t, SEG), BF16)
    f32 = jax.ShapeDtypeStruct(((t // rows_per_seq) * (gap + rows_per_seq) * ROW_SLOTS, HEAD_DIM), F32)
    out_specs, out_shape = [], []
    for s in segs:
        if _SEGMENTS[s][2]:
            out_specs.append(tall)
            out_shape.append(f32)
        out_specs.append(row(SEG))
        out_shape.append(bf)
    if gate:
        out_specs.append(row(B_HEADS))
        out_shape.append(jax.ShapeDtypeStruct((t, B_HEADS), F32))
    return pl.pallas_call(
        functools.partial(_proj_body, segs=tuple(segs), gate=gate),
        grid=(t // tm, n),
        in_specs=[row(d),
                  pl.BlockSpec((d, SEG), lambda i, j: (0, segs[0] + j)),
                  pl.BlockSpec((d, LANES), lambda i, j: (0, 0)),
                  pl.BlockSpec((1, LANES), lambda i, j: (0, 0)),
                  gain, gain, gain, gain, tab, tab, tab],
        out_specs=out_specs,
        out_shape=out_shape,
        scratch_shapes=[pltpu.VMEM((2, tm, SEG), F32)],
        compiler_params=_params("parallel", "arbitrary"),
        name="in_proj_" + "".join(str(s) for s in segs),
    )(h, w_main, w_f, b_f, g_qa.reshape(1, -1), g_ka.reshape(1, -1), g_qb.reshape(1, -1),
      g_kb.reshape(1, -1), *tabs)


def _fill_body(*refs):
    n = len(refs) // 3
    for src_ref, out_ref in zip(refs[n:2 * n], refs[2 * n:]):
        out_ref[...] = src_ref[...]


def _fill_gaps(dsts, srcs, *, n_seq):
    gap_rows = srcs[0].shape[0]
    blocks_per_seq = dsts[0].shape[0] // n_seq // gap_rows
    assert blocks_per_seq * gap_rows * n_seq == dsts[0].shape[0]
    n = len(dsts)
    return pl.pallas_call(
        _fill_body,
        grid=(n_seq,),
        in_specs=[pl.BlockSpec(memory_space=pl.ANY)] * n + [pl.BlockSpec(srcs[0].shape, lambda b: (0, 0))] * n,
        out_specs=[pl.BlockSpec((gap_rows, HEAD_DIM), lambda b: (b * blocks_per_seq, 0))] * n,
        out_shape=[jax.ShapeDtypeStruct(x.shape, x.dtype) for x in dsts],
        input_output_aliases={k: k for k in range(n)},
        compiler_params=_params("parallel"),
        name="fill_meta_rows",
    )(*dsts, *srcs)


def _cumsum_body(x_ref, upper_ref, earlier_ref, o_ref):
    sb, nb, _ = x_ref.shape
    hi = lax.Precision.HIGHEST
    x = x_ref[...].reshape(sb * nb, LANES)
    within = jnp.dot(x, upper_ref[...], precision=hi, preferred_element_type=F32)
    totals = jnp.broadcast_to(within[:, LANES - 1:LANES], within.shape)
    carry = jnp.dot(earlier_ref[...], totals, precision=hi, preferred_element_type=F32)
    o_ref[...] = (within + carry).reshape(sb, nb, LANES)


def _cumsum_lanes(x):
    s, length = x.shape
    lp = -(-length // (8 * LANES)) * (8 * LANES)
    nb = lp // LANES
    sb = _pick_tile(s, max(1, 512 // nb), 1)
    r = sb * nb
    xp = jnp.pad(x, ((0, 0), (0, lp - length))).reshape(s, nb, LANES)
    idx = np.arange(LANES)
    upper = jnp.asarray(idx[:, None] <= idx[None, :], F32)
    rid = np.arange(r)
    earlier = jnp.asarray((rid[None, :] < rid[:, None]) & (rid[None, :] // nb == rid[:, None] // nb), F32)
    blk = pl.BlockSpec((sb, nb, LANES), lambda i: (i, 0, 0))
    out = pl.pallas_call(
        _cumsum_body,
        grid=(s // sb,),
        in_specs=[blk, pl.BlockSpec((LANES, LANES), lambda i: (0, 0)), pl.BlockSpec((r, r), lambda i: (0, 0))],
        out_specs=blk,
        out_shape=jax.ShapeDtypeStruct((s, nb, LANES), F32),
        compiler_params=_params("parallel"),
        name="cumsum",
    )(xp, upper, earlier)
    return out.reshape(s, lp)[:, :length]


def _lam(lamv_ref, lam_init):
    v = lamv_ref[...]
    s1 = jnp.sum(v[0:1] * v[1:2], axis=-1, keepdims=True)
    s2 = jnp.sum(v[2:3] * v[3:4], axis=-1, keepdims=True)
    return jnp.exp(s1) - jnp.exp(s2) + lam_init


def _lanes(col):
    return jnp.broadcast_to(col, (col.shape[0], LANES))


def _wide(stat, width):
    return jnp.tile(stat, (1, width // LANES))


def _with_ones(v):
    return jnp.concatenate([v, jnp.ones((v.shape[0], LANES), v.dtype)], axis=1)


def _advance_max(ss, m_ref):
    m_prev = m_ref[...]
    m_new = m_prev
    for s in ss:
        m_new = jnp.maximum(m_new, jnp.max(s, axis=1)[:, None])
    m_ref[...] = m_new
    return m_new, jnp.exp2(m_prev - m_new)


def _diag_mask(s, chunk):
    rows = lax.broadcasted_iota(jnp.int32, s.shape, 0) // chunk
    cols = lax.broadcasted_iota(jnp.int32, s.shape, 1) // chunk
    return jnp.where(cols <= rows, s, NEG)


def _probs_init(s, m_ref):
    m = jnp.max(s, axis=1)[:, None]
    m_ref[...] = _lanes(m)
    return jnp.exp2(s - m)


def _attn_tile(seq):
    return _pick_tile(seq, 512, CHUNK)


def _key_tile(ref, kt, tk, cols):
    return ref[pl.ds(pl.multiple_of(kt * tk, tk), tk), cols]


def _pipelined_tiles(qi, step):
    def below_diagonal(i, carry):
        step(2 * i, False)
        step(2 * i + 1, False)
        return carry

    lax.fori_loop(0, qi // 2, below_diagonal, 0)

    @pl.when(qi % 2 == 1)
    def _():
        step(qi - 1, False)

    step(qi, True)


def _attn_a_body(q_ref, k_ref, v_ref, km_ref, vm_ref, lamv_ref, g_ref, o_ref, m_scr, l_scr, acc_scr, p_scr,
                 *, lam_init):
    qi = pl.program_id(2)
    tq = q_ref.shape[0]
    wide_v = 2 * HEAD_DIM
    cols = lambda u: slice(u * HEAD_DIM, (u + 1) * HEAD_DIM)
    vcols = lambda u: slice((u // 2) * wide_v, (u // 2 + 1) * wide_v)

    def lane_sums(p):
        return functools.reduce(lambda a, b: a + b, [p[:, k * LANES:(k + 1) * LANES] for k in range(p.shape[1] // LANES)])

    for u in range(PROMPT_UNITS):
        p = _probs_init(_dot_nt(q_ref[:, cols(u)], km_ref[:, cols(u)]), m_scr.at[u])
        l_scr[u] = _lanes(jnp.sum(p, axis=1)[:, None]) * (1.0 / LANES)
        acc_scr[u] = _dot(p.astype(BF16), vm_ref[:, vcols(u)])
        p_scr[u] = jnp.zeros(p_scr.shape[1:], BF16)

    def step(kt, diagonal):
        prev = jnp.maximum(kt - 1, 0)
        for u in range(PROMPT_UNITS):
            s = _dot_nt(q_ref[:, cols(u)], _key_tile(k_ref, kt, tq, cols(u)))
            if diagonal:
                s = _diag_mask(s, CHUNK)
            flushed = acc_scr[u] + _dot(p_scr[u], _key_tile(v_ref, prev, tq, vcols(u)))
            m_new, alpha = _advance_max([s], m_scr.at[u])
            p = jnp.exp2(s - _wide(m_new, tq))
            l_scr[u] = alpha * l_scr[u] + lane_sums(p)
            if diagonal:
                acc_scr[u] = _wide(alpha, wide_v) * flushed + _dot(p.astype(BF16), _key_tile(v_ref, kt, tq, vcols(u)))
            else:
                p_scr[u] = p.astype(BF16)
                acc_scr[u] = _wide(alpha, wide_v) * flushed

    _pipelined_tiles(qi, step)
    lam = _lam(lamv_ref, lam_init)
    for h in range(PROMPT_UNITS // 2):
        o0, o1 = [acc_scr[u] / _wide(_lanes(jnp.sum(l_scr[u], axis=1)[:, None]), wide_v) for u in (2 * h, 2 * h + 1)]
        o_ref[:, vcols(2 * h)] = (_rms(o0 - lam * o1, g_ref[...]) * (1.0 - lam_init)).astype(BF16)


def _attn_b_body(q_ref, k_ref, v_ref, km_ref, vm_ref, nc_ref, ncm_ref, g_ref, o_ref, m_scr, acc_scr, p_scr):
    qi = pl.program_id(2)
    tq = q_ref.shape[0]
    cols = lambda u: slice(u * HEAD_DIM, (u + 1) * HEAD_DIM)

    for u in range(PROMPT_UNITS):
        s = _dot_nt(q_ref[:, cols(u)], km_ref[:, cols(u)]) + ncm_ref[u:u + 1, :] * LOG2E
        p = _probs_init(s, m_scr.at[u])
        acc_scr[u] = _dot(p.astype(BF16), _with_ones(vm_ref[:, cols(u)]))
        p_scr[u] = jnp.zeros(p_scr.shape[1:], BF16)

    def step(kt, diagonal):
        prev = jnp.maximum(kt - 1, 0)
        for u in range(PROMPT_UNITS):
            s = _dot_nt(q_ref[:, cols(u)], _key_tile(k_ref, kt, tq, cols(u))) + nc_ref[u, pl.ds(kt, 1), :] * LOG2E
            if diagonal:
                s = _diag_mask(s, 1)
            flushed = acc_scr[u] + _dot(p_scr[u], _with_ones(_key_tile(v_ref, prev, tq, cols(u))))
            m_new, alpha = _advance_max([s], m_scr.at[u])
            p_scr[u] = jnp.exp2(s - _wide(m_new, tq)).astype(BF16)
            acc_scr[u] = _wide(alpha, 2 * HEAD_DIM) * flushed

    _pipelined_tiles(qi, step)
    for u in range(PROMPT_UNITS):
        acc = acc_scr[u] + _dot(p_scr[u], _with_ones(_key_tile(v_ref, qi, tq, cols(u))))
        o_ref[:, cols(u)] = _rms(acc[:, :HEAD_DIM] / acc[:, HEAD_DIM:], g_ref[...]).astype(BF16)


def _attn_prompt(mode, q, k, v, km, vm, extra, g_o, *, batch, seq, lam_init=None):
    tq = _attn_tile(seq)
    nq = seq // tq
    q_spec = pl.BlockSpec((tq, PROMPT_GROUP_W), lambda b, h, i: (b * nq + i, h))
    kv_spec = pl.BlockSpec((seq, PROMPT_GROUP_W), lambda b, h, i: (b, h))
    meta_spec = pl.BlockSpec((N_META, PROMPT_GROUP_W), lambda b, h, i: (0, h))
    stat = pltpu.VMEM((PROMPT_UNITS, tq, LANES), F32)
    acc = pltpu.VMEM((PROMPT_UNITS, tq, 2 * HEAD_DIM), F32)
    pend = pltpu.VMEM((PROMPT_UNITS, tq, tq), BF16)
    if mode == "a":
        lamv, = extra
        body = functools.partial(_attn_a_body, lam_init=lam_init)
        extra_specs = [pl.BlockSpec(lamv.shape, lambda b, h, i: (0, 0))]
        scratch = [stat, stat, acc, pend]
    else:
        negc, negc_meta = extra
        body = _attn_b_body
        extra_specs = [pl.BlockSpec((None, None, PROMPT_UNITS, nq, tq), lambda b, h, i: (b, h, 0, 0, 0)),
                       pl.BlockSpec((None, PROMPT_UNITS, N_META), lambda b, h, i: (h, 0, 0))]
        scratch = [stat, acc, pend]
    g_spec = pl.BlockSpec((1, g_o.shape[-1]), lambda b, h, i: (0, 0))
    return pl.pallas_call(
        body,
        grid=(batch, PROMPT_GROUPS, nq),
        in_specs=[q_spec, kv_spec, kv_spec, meta_spec, meta_spec] + extra_specs + [g_spec],
        out_specs=q_spec,
        out_shape=jax.ShapeDtypeStruct((batch * seq, SEG), BF16),
        scratch_shapes=scratch,
        compiler_params=_params("parallel", "parallel", "arbitrary"),
        name="attn_prompt_" + mode,
    )(q, k, v, km, vm, *extra, g_o.reshape(1, -1))


def _slot_rows(ref, slot, n):
    return ref[pl.ds(slot, n, stride=ROW_SLOTS), :]


def _attn_sample_body(qa_ref, kan_ref, van_ref, cka_ref, cva_ref, qb_ref, kbn_ref, vbn_ref, ckb_ref, cvb_ref,
                      nco_ref, ncn_ref, lamv_ref, goa_ref, gob_ref, oa_ref, ob_ref,
                      m_scr, l_scr, acca_scr, accb_scr, *, lam_init):
    j = pl.program_id(1)
    t = qa_ref.shape[0]
    tb = cka_ref.shape[0] // ROW_SLOTS
    wide_v = 2 * HEAD_DIM
    cols = lambda u: slice(u * HEAD_DIM, (u + 1) * HEAD_DIM)
    n_a, n_b = 2 * A_HEADS, B_HEADS

    @pl.when(j == 0)
    def _():
        m_scr[...] = jnp.full(m_scr.shape, NEG, F32)
        l_scr[...] = jnp.zeros_like(l_scr)
        acca_scr[...] = jnp.zeros_like(acca_scr)
        accb_scr[...] = jnp.zeros_like(accb_scr)

    def update_a(u, s, v):
        m_new, alpha = _advance_max([s], m_scr.at[u])
        p = jnp.exp2(s - m_new[:, :1])
        l_scr[u] = alpha * l_scr[u] + _lanes(jnp.sum(p, axis=1)[:, None])
        acca_scr[u] = _wide(alpha, wide_v) * acca_scr[u] + _dot(p.astype(BF16), v)

    def update_b(h, s, v):
        m_new, alpha = _advance_max([s], m_scr.at[n_a + h])
        p = jnp.exp2(s - m_new[:, :1])
        accb_scr[h] = _wide(alpha, wide_v) * accb_scr[h] + _dot(p.astype(BF16), _with_ones(v))

    for h in range(A_HEADS):
        v = jnp.concatenate([_slot_rows(cva_ref, half * A_HEADS + h, tb) for half in range(2)], axis=1).astype(BF16)
        for m in range(2):
            u = 2 * h + m
            update_a(u, _dot_nt(qa_ref[:, cols(u)], _slot_rows(cka_ref, u, tb).astype(BF16)), v)
    for h in range(n_b):
        s = _dot_nt(qb_ref[:, cols(h)], _slot_rows(ckb_ref, h, tb).astype(BF16)) + nco_ref[h, pl.ds(j, 1), :] * LOG2E
        update_b(h, s, _slot_rows(cvb_ref, h, tb).astype(BF16))

    @pl.when(j == pl.num_programs(1) - 1)
    def _():
        lam = _lam(lamv_ref, lam_init)
        for h in range(A_HEADS):
            for m in range(2):
                u = 2 * h + m
                update_a(u, _dot_nt(qa_ref[:, cols(u)], kan_ref[:, cols(u)]), van_ref[:, h * wide_v:(h + 1) * wide_v])
            o0, o1 = [acca_scr[u] / _wide(l_scr[u], wide_v) for u in (2 * h, 2 * h + 1)]
            oa_ref[:, h * wide_v:(h + 1) * wide_v] = (_rms(o0 - lam * o1, goa_ref[...]) * (1.0 - lam_init)).astype(BF16)
        for h in range(n_b):
            s = _dot_nt(qb_ref[:, cols(h)], kbn_ref[:, cols(h)]) + ncn_ref[h:h + 1, :] * LOG2E
            update_b(h, _diag_mask(s, 1), vbn_ref[:, cols(h)])
            acc = accb_scr[h]
            ob_ref[:, cols(h)] = _rms(acc[:, :HEAD_DIM] / acc[:, HEAD_DIM:], gob_ref[...]).astype(BF16)


def _attn_sample(qa, kan, van, cka, cva, qb, kbn, vbn, ckb, cvb, negc_old, negc_new, lamv, g_oa, g_ob, *,
                 batch, t, past, lam_init):
    nblk, tb = negc_old.shape[-2:]
    new = pl.BlockSpec((t, SEG), lambda b, j: (b, 0))
    old = pl.BlockSpec((tb * ROW_SLOTS, HEAD_DIM), lambda b, j: (b * nblk + j, 0))
    whole = lambda a: pl.BlockSpec(a.shape, lambda b, j: (0,) * a.ndim)
    goa, gob = g_oa.reshape(1, -1), g_ob.reshape(1, -1)
    out = jax.ShapeDtypeStruct((batch * t, SEG), BF16)
    n_a = 2 * A_HEADS
    return pl.pallas_call(
        functools.partial(_attn_sample_body, lam_init=lam_init),
        grid=(batch, nblk),
        in_specs=[new, new, new, old, old, new, new, new, old, old,
                  pl.BlockSpec((None, B_HEADS, nblk, tb), lambda b, j: (b, 0, 0, 0)),
                  pl.BlockSpec((None, B_HEADS, t), lambda b, j: (b, 0, 0)),
                  whole(lamv), whole(goa), whole(gob)],
        out_specs=[new, new],
        out_shape=[out, out],
        scratch_shapes=[pltpu.VMEM((n_a + B_HEADS, t, LANES), F32), pltpu.VMEM((n_a, t, LANES), F32),
                        pltpu.VMEM((n_a, t, 2 * HEAD_DIM), F32), pltpu.VMEM((B_HEADS, t, 2 * HEAD_DIM), F32)],
        compiler_params=_params("parallel", "arbitrary"),
        name="attn_sample",
    )(qa, kan, van, cka, cva, qb, kbn, vbn, ckb, cvb, negc_old, negc_new, lamv, goa, gob)


def _merge_body(x_ref, oa_ref, ob_ref, wa_ref, wb_ref, o_ref):
    o_ref[...] = x_ref[...] + _dot(oa_ref[...], wa_ref[...]) + _dot(ob_ref[...], wb_ref[...])


def _merge(x, oa, ob, w_a, w_b):
    t, d = x.shape
    tm = _pick_tile(t, 512, 16)
    row = lambda w: pl.BlockSpec((tm, w), lambda i: (i, 0))
    wsp = pl.BlockSpec((SEG, d), lambda i: (0, 0))
    return pl.pallas_call(
        _merge_body,
        grid=(t // tm,),
        in_specs=[row(d), row(SEG), row(SEG), wsp, wsp],
        out_specs=row(d),
        out_shape=jax.ShapeDtypeStruct((t, d), F32),
        compiler_params=_params("parallel"),
        name="out_proj",
    )(x, oa, ob, w_a, w_b)


def kernel(x_prompt, x_sample, cache_a_k, cache_a_v, cache_b_k, cache_b_v, cache_b_logf, meta_tokens, g_ffn1, ffn1_w1, ffn1_w3, ffn1_w2, g_mix, w_in, b_f, g_qa, g_ka, g_qb, g_kb, lambda_q1, lambda_k1, lambda_q2, lambda_k2, g_oa, g_ob, w_out, g_ffn2, ffn2_w1, ffn2_w3, ffn2_w2, g_final):
    depth = w_in.shape[0]
    assert depth == 1, "meta rows skip attention, which is only valid for a single layer"
    bsz, seq, d = x_prompt.shape
    dbsz, dseq, _ = x_sample.shape
    past = cache_a_k.shape[2]
    n_small = dbsz * dseq
    lam_init = 0.8 - 0.6 * math.exp(-0.3 * 0)
    bf = lambda a: a.astype(BF16)

    w_main = bf(w_in[0, :, :N_SEG * SEG])
    w_f = bf(jnp.pad(w_in[0, :, N_SEG * SEG:], ((0, 0), (0, LANES - B_HEADS))))
    b_fp = jnp.pad(b_f[0], (0, LANES - B_HEADS)).reshape(1, LANES)
    lamv = jnp.stack([lambda_q1[0], lambda_k1[0], lambda_q2[0], lambda_k2[0]]).astype(F32)
    wo_a, wo_b = bf(w_out[0, :SEG]), bf(w_out[0, SEG:])
    f1 = (g_ffn1[0], bf(ffn1_w1[0]), bf(ffn1_w3[0]), bf(ffn1_w2[0]), g_mix[0])
    f2 = (g_ffn2[0], bf(ffn2_w1[0]), bf(ffn2_w3[0]), bf(ffn2_w2[0]), g_final[0])

    xp = x_prompt.reshape(bsz * seq, d)
    n_pad = -(n_small + N_META) % SMALL_ROW_MULT
    xs = jnp.concatenate([x_sample.reshape(n_small, d), meta_tokens.astype(x_sample.dtype),
                          jnp.zeros((n_pad, d), x_sample.dtype)], axis=0)
    pos_p = N_META + jnp.arange(seq)
    pos_s = jnp.concatenate([jnp.tile(past + jnp.arange(dseq), dbsz), jnp.arange(N_META),
                             jnp.zeros((n_pad,), jnp.int32)])

    x1p, hp = _ffn(xp, *f1, mode="emit_norm")
    x1s, hs = _ffn(xs, *f1, mode="emit_norm")
    proj = functools.partial(_project, w_main=w_main, w_f=w_f, b_f=b_fp, g_qa=g_qa[0], g_ka=g_ka[0],
                             g_qb=g_qb[0], g_kb=g_kb[0])
    diff_segs, fox_segs = (0, 1, 2), (3, 4, 5)
    qa_p, kaf_p, kab_p, vaf_p, vab_p, logf_p = proj(hp, pos=pos_p, segs=diff_segs, gate=True, rows_per_seq=seq, gap=N_META)
    qb_p, kbf_p, kbb_p, vbf_p, vbb_p = proj(hp, pos=pos_p, segs=fox_segs, gate=False, rows_per_seq=seq, gap=N_META)
    qa_s, kaf_s, kab_s, vaf_s, vab_s, logf_s = proj(hs, pos=pos_s, segs=diff_segs, gate=True, rows_per_seq=xs.shape[0])
    qb_s, kbf_s, kbb_s, vbf_s, vbb_s = proj(hs, pos=pos_s, segs=fox_segs, gate=False, rows_per_seq=xs.shape[0])

    logf_meta = logf_s[n_small:n_small + N_META]
    c_real = _cumsum_lanes(logf_p.reshape(bsz, seq, B_HEADS).transpose(0, 2, 1).reshape(bsz * B_HEADS, seq))
    c_meta = _cumsum_lanes(logf_meta.T)
    seq_s = jnp.concatenate([cache_b_logf[0].astype(F32), logf_s[:n_small].reshape(dbsz, dseq, B_HEADS)], axis=1)
    c_small = _cumsum_lanes(seq_s.transpose(0, 2, 1).reshape(dbsz * B_HEADS, past + dseq))
    tq = _attn_tile(seq)
    negc_p = (-c_real).reshape(bsz, PROMPT_GROUPS, PROMPT_UNITS, seq // tq, tq)
    negc_meta = (c_meta[:, N_META - 1:] - c_meta).reshape(PROMPT_GROUPS, PROMPT_UNITS, N_META)
    tb = _pick_tile(past, SAMPLE_BLOCK, LANES)
    negc_s_old = (-c_small[:, :past]).reshape(dbsz, B_HEADS, past // tb, tb)
    negc_s_new = (-c_small[:, past:]).reshape(dbsz, B_HEADS, dseq)

    meta = slice(n_small, n_small + N_META)
    oa_p = _attn_prompt("a", qa_p, kab_p, vab_p, kab_s[meta], vab_s[meta], (lamv,), g_oa[0],
                        batch=bsz, seq=seq, lam_init=lam_init)
    ob_p = _attn_prompt("b", qb_p, kbb_p, vbb_p, kbb_s[meta], vbb_s[meta], (negc_p, negc_meta), g_ob[0],
                        batch=bsz, seq=seq)
    rows = lambda c: c[0].reshape(-1, HEAD_DIM)
    cva = cache_a_v[0].reshape(dbsz, past, A_HEADS, 2, HEAD_DIM).swapaxes(2, 3).reshape(-1, HEAD_DIM)
    oa_s, ob_s = _attn_sample(
        qa_s, kab_s, vab_s, rows(cache_a_k), cva, qb_s, kbb_s, vbb_s, rows(cache_b_k), rows(cache_b_v),
        negc_s_old, negc_s_new, lamv, g_oa[0], g_ob[0], batch=dbsz, t=dseq, past=past, lam_init=lam_init)

    y_p = _ffn(_merge(x1p, oa_p, ob_p, wo_a, wo_b), *f2, mode="final_norm")
    y_s = _ffn(_merge(x1s[:n_small], oa_s, ob_s, wo_a, wo_b), *f2, mode="final_norm")

    def unslot(x, lead, tail, slots=None):
        if slots == VA_SLOTS:
            x = x.reshape(lead + (2, A_HEADS, HEAD_DIM)).swapaxes(-3, -2)
        return x.reshape(lead + tail)

    small_f32 = (kaf_s, vaf_s, kbf_s, vbf_s)
    meta_rows = [x[n_small * ROW_SLOTS:(n_small + N_META) * ROW_SLOTS] for x in small_f32]
    full = _fill_gaps((kaf_p, vaf_p, kbf_p, vbf_p), meta_rows, n_seq=bsz)
    tails = ((A_HEADS, 2, HEAD_DIM), (A_HEADS, 2 * HEAD_DIM), (B_HEADS, HEAD_DIM), (B_HEADS, HEAD_DIM))
    slots = (None, VA_SLOTS, None, None)
    ak_p, av_p, bk_p, bv_p = [unslot(x, (1, bsz, N_META + seq), tl, sl) for x, tl, sl in zip(full, tails, slots)]
    ak_s, av_s, bk_s, bv_s = [unslot(x[:n_small * ROW_SLOTS], (1, dbsz, dseq), tl, sl)
                              for x, tl, sl in zip(small_f32, tails, slots)]
    lf_meta = jnp.broadcast_to(logf_meta[None], (bsz, N_META, B_HEADS))
    lf_p = jnp.concatenate([lf_meta, logf_p.reshape(bsz, seq, B_HEADS)], axis=1)[None]
    lf_s = logf_s[:n_small].reshape(1, dbsz, dseq, B_HEADS)
    return (y_p.reshape(bsz, seq, d), y_s.reshape(dbsz, dseq, d),
            ak_p, av_p, bk_p, bv_p, lf_p, ak_s, av_s, bk_s, bv_s, lf_s)
```

```python
import functools
import math

import jax
import jax.numpy as jnp
import numpy as np
from jax import lax
from jax.experimental import pallas as pl
from jax.experimental.pallas import tpu as pltpu

F32 = jnp.float32
BF16 = jnp.bfloat16

HEAD_DIM = 128
A_HEADS = 4
B_HEADS = 8
SEG = 1024
N_SEG = 6
PROMPT_UNITS = 4
PROMPT_GROUP_W = PROMPT_UNITS * HEAD_DIM
PROMPT_GROUPS = SEG // PROMPT_GROUP_W
ROW_SLOTS = SEG // HEAD_DIM
VA_SLOTS = [(c % 2) * A_HEADS + c // 2 for c in range(ROW_SLOTS)]
N_META = 16
CHUNK = 64
ROPE_DIM = HEAD_DIM // 4
ROPE_THETA = 500000.0
EPS = 1e-6
NEG = -1e30
LOG2E = math.log2(math.e)
LANES = 128
VMEM_LIMIT = 63 * 1024 * 1024
SAMPLE_BLOCK = 1024
SMALL_ROW_MULT = 16


def _params(*semantics):
    return pltpu.CompilerParams(dimension_semantics=semantics, vmem_limit_bytes=VMEM_LIMIT)


def _pick_tile(n, target, mult):
    best = None
    for t in range(mult, min(n, target) + 1, mult):
        if n % t == 0:
            best = t
    assert best is not None, (n, target, mult)
    return best


def _rms(x, g):
    ms = jnp.mean(x * x, axis=-1, keepdims=True)
    return (x * lax.rsqrt(ms + EPS)) * g


def _dot(a, b):
    return jnp.dot(a, b, preferred_element_type=F32)


def _dot_nt(a, b):
    return lax.dot_general(a, b, (((1,), (1,)), ((), ())), preferred_element_type=F32)


def _ffn_body(x_ref, g_ref, w1_ref, w3_ref, w2_ref, gout_ref, *rest, mode):
    if mode == "emit_norm":
        out_ref, h_ref, xn_scr = rest
    else:
        out_ref, xn_scr = rest
    f = pl.program_id(1)

    @pl.when(f == 0)
    def _():
        xn_scr[...] = _rms(x_ref[...], g_ref[...]).astype(BF16)
        out_ref[...] = jnp.zeros_like(out_ref)

    xn = xn_scr[...]
    h1 = _dot(xn, w1_ref[...])
    h3 = _dot(xn, w3_ref[...])
    gate = (h1 * jax.nn.sigmoid(h1)) * h3
    out_ref[...] += _dot(gate.astype(BF16), w2_ref[...])

    @pl.when(f == pl.num_programs(1) - 1)
    def _():
        y = x_ref[...] + 0.5 * out_ref[...]
        if mode == "emit_norm":
            out_ref[...] = y
            h_ref[...] = _rms(y, gout_ref[...]).astype(BF16)
        else:
            out_ref[...] = _rms(y, gout_ref[...])


def _ffn(x, g_in, w1, w3, w2, g_out, *, mode):
    t, d = x.shape
    f = w1.shape[1]
    tm = _pick_tile(t, 1024, 16)
    tf = _pick_tile(f, 512, LANES)
    row = pl.BlockSpec((tm, d), lambda i, j: (i, 0))
    vec = pl.BlockSpec((1, d), lambda i, j: (0, 0))
    out_shape = [jax.ShapeDtypeStruct((t, d), F32)]
    out_specs = [row]
    x_spec = row
    if mode == "emit_norm":
        out_shape.append(jax.ShapeDtypeStruct((t, d), BF16))
        out_specs.append(row)
        x_spec = pl.BlockSpec((tm, d), lambda i, j: (i, 0), pipeline_mode=pl.Buffered(1))
    res = pl.pallas_call(
        functools.partial(_ffn_body, mode=mode),
        grid=(t // tm, f // tf),
        in_specs=[x_spec, vec,
                  pl.BlockSpec((d, tf), lambda i, j: (0, j)),
                  pl.BlockSpec((d, tf), lambda i, j: (0, j)),
                  pl.BlockSpec((tf, d), lambda i, j: (j, 0)),
                  vec],
        out_specs=out_specs,
        out_shape=out_shape,
        scratch_shapes=[pltpu.VMEM((tm, d), BF16)],
        compiler_params=_params("parallel", "arbitrary"),
        name="ffn_" + mode,
    )(x, g_in.reshape(1, d), w1, w3, w2, g_out.reshape(1, d))
    return res if mode == "emit_norm" else res[0]


_NATURAL = list(range(ROW_SLOTS))
_SEGMENTS = ((0, True, False, None, True),
             (1, True, True, _NATURAL, False),
             (None, False, True, VA_SLOTS, False),
             (2, False, False, None, True),
             (3, False, True, _NATURAL, False),
             (None, False, True, _NATURAL, False))


def _proj_body(*refs, segs, gate):
    h_ref, w_ref, wf_ref, bf_ref = refs[:4]
    gain_refs = refs[4:8]
    tc_ref, ta_ref, tb_ref = refs[8:11]
    outs, acc_scr = list(refs[11:-1]), refs[-1]
    seg_outs = [(outs.pop(0) if _SEGMENTS[s][2] else None, outs.pop(0)) for s in segs]
    j = pl.program_id(1)
    q_scale = HEAD_DIM ** -0.5 * LOG2E

    def finish(k):
        gain, rope, _, slots, is_query = _SEGMENTS[segs[k]]
        f32_ref, bf_ref_ = seg_outs[k]
        for c in range(ROW_SLOTS):
            sl = slice(c * HEAD_DIM, (c + 1) * HEAD_DIM)
            y = acc_scr[k % 2, :, sl]
            if gain is not None:
                y = _rms(y, gain_refs[gain][...])
            if rope:
                y = (y * tc_ref[...] + pltpu.roll(y, HEAD_DIM - ROPE_DIM // 2, 1) * ta_ref[...]
                     + pltpu.roll(y, ROPE_DIM // 2, 1) * tb_ref[...])
            if f32_ref is not None:
                f32_ref[pl.ds(slots[c], y.shape[0], stride=ROW_SLOTS), :] = y
            bf_ref_[:, sl] = (y * q_scale if is_query else y).astype(BF16)

    for k in range(len(segs)):
        @pl.when(j == k)
        def _(k=k):
            acc_scr[k % 2] = _dot(h_ref[...], w_ref[...])
            if k == 0 and gate:
                z = _dot(h_ref[...], wf_ref[...]) + bf_ref[...]
                logf = jnp.minimum(z, 0.0) - jnp.log1p(jnp.exp(-jnp.abs(z)))
                outs[0][...] = logf[:, :B_HEADS]
            if k > 0:
                finish(k - 1)
            if k == len(segs) - 1:
                finish(k)


def _rope_tables(pos):
    half = ROPE_DIM // 2
    inv = jnp.power(ROPE_THETA, -jnp.arange(half, dtype=F32) * 2.0 / ROPE_DIM)
    ang = pos.astype(F32)[:, None] * inv[None, :]
    cos, sin = jnp.cos(ang), jnp.sin(ang)
    n = pos.shape[0]
    zeros = jnp.zeros((n, HEAD_DIM - ROPE_DIM), F32)
    tab_c = jnp.concatenate([cos, cos, jnp.ones((n, HEAD_DIM - ROPE_DIM), F32)], axis=1)
    tab_a = jnp.concatenate([-sin, jnp.zeros((n, half), F32), zeros], axis=1)
    tab_b = jnp.concatenate([jnp.zeros((n, half), F32), sin, zeros], axis=1)
    return tab_c, tab_a, tab_b


def _project(h, w_main, w_f, b_f, g_qa, g_ka, g_qb, g_kb, pos, *, segs, gate, rows_per_seq, gap=0):
    t, d = h.shape
    tm = _pick_tile(rows_per_seq, 1024, 16)
    tiles_per_seq = rows_per_seq // tm
    n = len(segs)
    tabs = _rope_tables(pos)
    row = lambda w: pl.BlockSpec((tm, w), lambda i, j: (i, 0))
    first_row = lambda i: ((i // tiles_per_seq) * (gap + rows_per_seq) + gap + (i % tiles_per_seq) * tm) * ROW_SLOTS
    tall = pl.BlockSpec((pl.Element(tm * ROW_SLOTS), pl.Element(HEAD_DIM)), lambda i, j: (first_row(i), 0))
    gain = pl.BlockSpec((1, HEAD_DIM), lambda i, j: (0, 0))
    tab = pl.BlockSpec((tm, HEAD_DIM), lambda i, j: (i % tiles_per_seq, 0))
    bf = jax.ShapeDtypeStruct((t, SEG), BF16)
    f32 = jax.ShapeDtypeStruct(((t // rows_per_seq) * (gap + rows_per_seq) * ROW_SLOTS, HEAD_DIM), F32)
    out_specs, out_shape = [], []
    for s in segs:
        if _SEGMENTS[s][2]:
            out_specs.append(tall)
            out_shape.append(f32)
        out_specs.append(row(SEG))
        out_shape.append(bf)
    if gate:
        out_specs.append(row(B_HEADS))
        out_shape.append(jax.ShapeDtypeStruct((t, B_HEADS), F32))
    return pl.pallas_call(
        functools.partial(_proj_body, segs=tuple(segs), gate=gate),
        grid=(t // tm, n),
        in_specs=[row(d),
                  pl.BlockSpec((d, SEG), lambda i, j: (0, segs[0] + j)),
                  pl.BlockSpec((d, LANES), lambda i, j: (0, 0)),
                  pl.BlockSpec((1, LANES), lambda i, j: (0, 0)),
                  gain, gain, gain, gain, tab, tab, tab],
        out_specs=out_specs,
        out_shape=out_shape,
        scratch_shapes=[pltpu.VMEM((2, tm, SEG), F32)],
        compiler_params=_params("parallel", "arbitrary"),
        name="in_proj_" + "".join(str(s) for s in segs),
    )(h, w_main, w_f, b_f, g_qa.reshape(1, -1), g_ka.reshape(1, -1), g_qb.reshape(1, -1),
      g_kb.reshape(1, -1), *tabs)


def _fill_body(*refs):
    n = len(refs) // 3
    for src_ref, out_ref in zip(refs[n:2 * n], refs[2 * n:]):
        out_ref[...] = src_ref[...]


def _fill_gaps(dsts, srcs, *, n_seq):
    gap_rows = srcs[0].shape[0]
    blocks_per_seq = dsts[0].shape[0] // n_seq // gap_rows
    assert blocks_per_seq * gap_rows * n_seq == dsts[0].shape[0]
    n = len(dsts)
    return pl.pallas_call(
        _fill_body,
        grid=(n_seq,),
        in_specs=[pl.BlockSpec(memory_space=pl.ANY)] * n + [pl.BlockSpec(srcs[0].shape, lambda b: (0, 0))] * n,
        out_specs=[pl.BlockSpec((gap_rows, HEAD_DIM), lambda b: (b * blocks_per_seq, 0))] * n,
        out_shape=[jax.ShapeDtypeStruct(x.shape, x.dtype) for x in dsts],
        input_output_aliases={k: k for k in range(n)},
        compiler_params=_params("parallel"),
        name="fill_meta_rows",
    )(*dsts, *srcs)


def _cumsum_body(x_ref, upper_ref, earlier_ref, o_ref):
    sb, nb, _ = x_ref.shape
    hi = lax.Precision.HIGHEST
    x = x_ref[...].reshape(sb * nb, LANES)
    within = jnp.dot(x, upper_ref[...], precision=hi, preferred_element_type=F32)
    totals = jnp.broadcast_to(within[:, LANES - 1:LANES], within.shape)
    carry = jnp.dot(earlier_ref[...], totals, precision=hi, preferred_element_type=F32)
    o_ref[...] = (within + carry).reshape(sb, nb, LANES)


def _cumsum_lanes(x):
    s, length = x.shape
    lp = -(-length // (8 * LANES)) * (8 * LANES)
    nb = lp // LANES
    sb = _pick_tile(s, max(1, 512 // nb), 1)
    r = sb * nb
    xp = jnp.pad(x, ((0, 0), (0, lp - length))).reshape(s, nb, LANES)
    idx = np.arange(LANES)
    upper = jnp.asarray(idx[:, None] <= idx[None, :], F32)
    rid = np.arange(r)
    earlier = jnp.asarray((rid[None, :] < rid[:, None]) & (rid[None, :] // nb == rid[:, None] // nb), F32)
    blk = pl.BlockSpec((sb, nb, LANES), lambda i: (i, 0, 0))
    out = pl.pallas_call(
        _cumsum_body,
        grid=(s // sb,),
        in_specs=[blk, pl.BlockSpec((LANES, LANES), lambda i: (0, 0)), pl.BlockSpec((r, r), lambda i: (0, 0))],
        out_specs=blk,
        out_shape=jax.ShapeDtypeStruct((s, nb, LANES), F32),
        compiler_params=_params("parallel"),
        name="cumsum",
    )(xp, upper, earlier)
    return out.reshape(s, lp)[:, :length]


def _lam(lamv_ref, lam_init):
    v = lamv_ref[...]
    s1 = jnp.sum(v[0:1] * v[1:2], axis=-1, keepdims=True)
    s2 = jnp.sum(v[2:3] * v[3:4], axis=-1, keepdims=True)
    return jnp.exp(s1) - jnp.exp(s2) + lam_init


def _lanes(col):
    return jnp.broadcast_to(col, (col.shape[0], LANES))


def _wide(stat, width):
    return jnp.tile(stat, (1, width // LANES))


def _with_ones(v):
    return jnp.concatenate([v, jnp.ones((v.shape[0], LANES), v.dtype)], axis=1)


def _advance_max(ss, m_ref):
    m_prev = m_ref[...]
    m_new = m_prev
    for s in ss:
        m_new = jnp.maximum(m_new, jnp.max(s, axis=1)[:, None])
    m_ref[...] = m_new
    return m_new, jnp.exp2(m_prev - m_new)


def _diag_mask(s, chunk):
    rows = lax.broadcasted_iota(jnp.int32, s.shape, 0) // chunk
    cols = lax.broadcasted_iota(jnp.int32, s.shape, 1) // chunk
    return jnp.where(cols <= rows, s, NEG)


def _probs_init(s, m_ref):
    m = jnp.max(s, axis=1)[:, None]
    m_ref[...] = _lanes(m)
    return jnp.exp2(s - m)


def _attn_tile(seq):
    return _pick_tile(seq, 512, CHUNK)


def _key_tile(ref, kt, tk, cols):
    return ref[pl.ds(pl.multiple_of(kt * tk, tk), tk), cols]


def _pipelined_tiles(qi, step):
    def below_diagonal(i, carry):
        step(2 * i, False)
        step(2 * i + 1, False)
        return carry

    lax.fori_loop(0, qi // 2, below_diagonal, 0)

    @pl.when(qi % 2 == 1)
    def _():
        step(qi - 1, False)

    step(qi, True)


def _attn_a_body(q_ref, k_ref, v_ref, km_ref, vm_ref, lamv_ref, g_ref, o_ref, m_scr, l_scr, acc_scr, p_scr,
                 *, lam_init):
    qi = pl.program_id(2)
    tq = q_ref.shape[0]
    wide_v = 2 * HEAD_DIM
    cols = lambda u: slice(u * HEAD_DIM, (u + 1) * HEAD_DIM)
    vcols = lambda u: slice((u // 2) * wide_v, (u // 2 + 1) * wide_v)

    def lane_sums(p):
        return functools.reduce(lambda a, b: a + b, [p[:, k * LANES:(k + 1) * LANES] for k in range(p.shape[1] // LANES)])

    for u in range(PROMPT_UNITS):
        p = _probs_init(_dot_nt(q_ref[:, cols(u)], km_ref[:, cols(u)]), m_scr.at[u])
        l_scr[u] = _lanes(jnp.sum(p, axis=1)[:, None]) * (1.0 / LANES)
        acc_scr[u] = _dot(p.astype(BF16), vm_ref[:, vcols(u)])
        p_scr[u] = jnp.zeros(p_scr.shape[1:], BF16)

    def step(kt, diagonal):
        prev = jnp.maximum(kt - 1, 0)
        for u in range(PROMPT_UNITS):
            s = _dot_nt(q_ref[:, cols(u)], _key_tile(k_ref, kt, tq, cols(u)))
            if diagonal:
                s = _diag_mask(s, CHUNK)
            flushed = acc_scr[u] + _dot(p_scr[u], _key_tile(v_ref, prev, tq, vcols(u)))
            m_new, alpha = _advance_max([s], m_scr.at[u])
            p = jnp.exp2(s - _wide(m_new, tq))
            l_scr[u] = alpha * l_scr[u] + lane_sums(p)
            if diagonal:
                acc_scr[u] = _wide(alpha, wide_v) * flushed + _dot(p.astype(BF16), _key_tile(v_ref, kt, tq, vcols(u)))
            else:
                p_scr[u] = p.astype(BF16)
                acc_scr[u] = _wide(alpha, wide_v) * flushed

    _pipelined_tiles(qi, step)
    lam = _lam(lamv_ref, lam_init)
    for h in range(PROMPT_UNITS // 2):
        o0, o1 = [acc_scr[u] / _wide(_lanes(jnp.sum(l_scr[u], axis=1)[:, None]), wide_v) for u in (2 * h, 2 * h + 1)]
        o_ref[:, vcols(2 * h)] = (_rms(o0 - lam * o1, g_ref[...]) * (1.0 - lam_init)).astype(BF16)


def _attn_b_body(q_ref, k_ref, v_ref, km_ref, vm_ref, nc_ref, ncm_ref, g_ref, o_ref, m_scr, acc_scr, p_scr):
    qi = pl.program_id(2)
    tq = q_ref.shape[0]
    cols = lambda u: slice(u * HEAD_DIM, (u + 1) * HEAD_DIM)

    for u in range(PROMPT_UNITS):
        s = _dot_nt(q_ref[:, cols(u)], km_ref[:, cols(u)]) + ncm_ref[u:u + 1, :] * LOG2E
        p = _probs_init(s, m_scr.at[u])
        acc_scr[u] = _dot(p.astype(BF16), _with_ones(vm_ref[:, cols(u)]))
        p_scr[u] = jnp.zeros(p_scr.shape[1:], BF16)

    def step(kt, diagonal):
        prev = jnp.maximum(kt - 1, 0)
        for u in range(PROMPT_UNITS):
            s = _dot_nt(q_ref[:, cols(u)], _key_tile(k_ref, kt, tq, cols(u))) + nc_ref[u, pl.ds(kt, 1), :] * LOG2E
            if diagonal:
                s = _diag_mask(s, 1)
            flushed = acc_scr[u] + _dot(p_scr[u], _with_ones(_key_tile(v_ref, prev, tq, cols(u))))
            m_new, alpha = _advance_max([s], m_scr.at[u])
            p_scr[u] = jnp.exp2(s - _wide(m_new, tq)).astype(BF16)
            acc_scr[u] = _wide(alpha, 2 * HEAD_DIM) * flushed

    _pipelined_tiles(qi, step)
    for u in range(PROMPT_UNITS):
        acc = acc_scr[u] + _dot(p_scr[u], _with_ones(_key_tile(v_ref, qi, tq, cols(u))))
        o_ref[:, cols(u)] = _rms(acc[:, :HEAD_DIM] / acc[:, HEAD_DIM:], g_ref[...]).astype(BF16)


def _attn_prompt(mode, q, k, v, km, vm, extra, g_o, *, batch, seq, lam_init=None):
    tq = _attn_tile(seq)
    nq = seq // tq
    q_spec = pl.BlockSpec((tq, PROMPT_GROUP_W), lambda b, h, i: (b * nq + i, h))
    kv_spec = pl.BlockSpec((seq, PROMPT_GROUP_W), lambda b, h, i: (b, h))
    meta_spec = pl.BlockSpec((N_META, PROMPT_GROUP_W), lambda b, h, i: (0, h))
    stat = pltpu.VMEM((PROMPT_UNITS, tq, LANES), F32)
    acc = pltpu.VMEM((PROMPT_UNITS, tq, 2 * HEAD_DIM), F32)
    pend = pltpu.VMEM((PROMPT_UNITS, tq, tq), BF16)
    if mode == "a":
        lamv, = extra
        body = functools.partial(_attn_a_body, lam_init=lam_init)
        extra_specs = [pl.BlockSpec(lamv.shape, lambda b, h, i: (0, 0))]
        scratch = [stat, stat, acc, pend]
    else:
        negc, negc_meta = extra
        body = _attn_b_body
        extra_specs = [pl.BlockSpec((None, None, PROMPT_UNITS, nq, tq), lambda b, h, i: (b, h, 0, 0, 0)),
                       pl.BlockSpec((None, PROMPT_UNITS, N_META), lambda b, h, i: (h, 0, 0))]
        scratch = [stat, acc, pend]
    g_spec = pl.BlockSpec((1, g_o.shape[-1]), lambda b, h, i: (0, 0))
    return pl.pallas_call(
        body,
        grid=(batch, PROMPT_GROUPS, nq),
        in_specs=[q_spec, kv_spec, kv_spec, meta_spec, meta_spec] + extra_specs + [g_spec],
        out_specs=q_spec,
        out_shape=jax.ShapeDtypeStruct((batch * seq, SEG), BF16),
        scratch_shapes=scratch,
        compiler_params=_params("parallel", "parallel", "arbitrary"),
        name="attn_prompt_" + mode,
    )(q, k, v, km, vm, *extra, g_o.reshape(1, -1))


def _slot_rows(ref, slot, n):
    return ref[pl.ds(slot, n, stride=ROW_SLOTS), :]


def _attn_sample_body(qa_ref, kan_ref, van_ref, cka_ref, cva_ref, qb_ref, kbn_ref, vbn_ref, ckb_ref, cvb_ref,
                      nco_ref, ncn_ref, lamv_ref, goa_ref, gob_ref, oa_ref, ob_ref,
                      m_scr, l_scr, acca_scr, accb_scr, *, lam_init):
    j = pl.program_id(1)
    t = qa_ref.shape[0]
    tb = cka_ref.shape[0] // ROW_SLOTS
    wide_v = 2 * HEAD_DIM
    cols = lambda u: slice(u * HEAD_DIM, (u + 1) * HEAD_DIM)
    n_a, n_b = 2 * A_HEADS, B_HEADS

    @pl.when(j == 0)
    def _():
        m_scr[...] = jnp.full(m_scr.shape, NEG, F32)
        l_scr[...] = jnp.zeros_like(l_scr)
        acca_scr[...] = jnp.zeros_like(acca_scr)
        accb_scr[...] = jnp.zeros_like(accb_scr)

    def update_a(u, s, v):
        m_new, alpha = _advance_max([s], m_scr.at[u])
        p = jnp.exp2(s - m_new[:, :1])
        l_scr[u] = alpha * l_scr[u] + _lanes(jnp.sum(p, axis=1)[:, None])
        acca_scr[u] = _wide(alpha, wide_v) * acca_scr[u] + _dot(p.astype(BF16), v)

    def update_b(h, s, v):
        m_new, alpha = _advance_max([s], m_scr.at[n_a + h])
        p = jnp.exp2(s - m_new[:, :1])
        accb_scr[h] = _wide(alpha, wide_v) * accb_scr[h] + _dot(p.astype(BF16), _with_ones(v))

    for h in range(A_HEADS):
        v = jnp.concatenate([_slot_rows(cva_ref, half * A_HEADS + h, tb) for half in range(2)], axis=1).astype(BF16)
        for m in range(2):
            u = 2 * h + m
            update_a(u, _dot_nt(qa_ref[:, cols(u)], _slot_rows(cka_ref, u, tb).astype(BF16)), v)
    for h in range(n_b):
        s = _dot_nt(qb_ref[:, cols(h)], _slot_rows(ckb_ref, h, tb).astype(BF16)) + nco_ref[h, pl.ds(j, 1), :] * LOG2E
        update_b(h, s, _slot_rows(cvb_ref, h, tb).astype(BF16))

    @pl.when(j == pl.num_programs(1) - 1)
    def _():
        lam = _lam(lamv_ref, lam_init)
        for h in range(A_HEADS):
            for m in range(2):
                u = 2 * h + m
                update_a(u, _dot_nt(qa_ref[:, cols(u)], kan_ref[:, cols(u)]), van_ref[:, h * wide_v:(h + 1) * wide_v])
            o0, o1 = [acca_scr[u] / _wide(l_scr[u], wide_v) for u in (2 * h, 2 * h + 1)]
            oa_ref[:, h * wide_v:(h + 1) * wide_v] = (_rms(o0 - lam * o1, goa_ref[...]) * (1.0 - lam_init)).astype(BF16)
        for h in range(n_b):
            s = _dot_nt(qb_ref[:, cols(h)], kbn_ref[:, cols(h)]) + ncn_ref[h:h + 1, :] * LOG2E
            update_b(h, _diag_mask(s, 1), vbn_ref[:, cols(h)])
            acc = accb_scr[h]
            ob_ref[:, cols(h)] = _rms(acc[:, :HEAD_DIM] / acc[:, HEAD_DIM:], gob_ref[...]).astype(BF16)


def _attn_sample(qa, kan, van, cka, cva, qb, kbn, vbn, ckb, cvb, negc_old, negc_new, lamv, g_oa, g_ob, *,
                 batch, t, past, lam_init):
    nblk, tb = negc_old.shape[-2:]
    new = pl.BlockSpec((t, SEG), lambda b, j: (b, 0))
    old = pl.BlockSpec((tb * ROW_SLOTS, HEAD_DIM), lambda b, j: (b * nblk + j, 0))
    whole = lambda a: pl.BlockSpec(a.shape, lambda b, j: (0,) * a.ndim)
    goa, gob = g_oa.reshape(1, -1), g_ob.reshape(1, -1)
    out = jax.ShapeDtypeStruct((batch * t, SEG), BF16)
    n_a = 2 * A_HEADS
    return pl.pallas_call(
        functools.partial(_attn_sample_body, lam_init=lam_init),
        grid=(batch, nblk),
        in_specs=[new, new, new, old, old, new, new, new, old, old,
                  pl.BlockSpec((None, B_HEADS, nblk, tb), lambda b, j: (b, 0, 0, 0)),
                  pl.BlockSpec((None, B_HEADS, t), lambda b, j: (b, 0, 0)),
                  whole(lamv), whole(goa), whole(gob)],
        out_specs=[new, new],
        out_shape=[out, out],
        scratch_shapes=[pltpu.VMEM((n_a + B_HEADS, t, LANES), F32), pltpu.VMEM((n_a, t, LANES), F32),
                        pltpu.VMEM((n_a, t, 2 * HEAD_DIM), F32), pltpu.VMEM((B_HEADS, t, 2 * HEAD_DIM), F32)],
        compiler_params=_params("parallel", "arbitrary"),
        name="attn_sample",
    )(qa, kan, van, cka, cva, qb, kbn, vbn, ckb, cvb, negc_old, negc_new, lamv, goa, gob)


def _merge_body(x_ref, oa_ref, ob_ref, wa_ref, wb_ref, o_ref):
    o_ref[...] = x_ref[...] + _dot(oa_ref[...], wa_ref[...]) + _dot(ob_ref[...], wb_ref[...])


def _merge(x, oa, ob, w_a, w_b):
    t, d = x.shape
    tm = _pick_tile(t, 512, 16)
    row = lambda w: pl.BlockSpec((tm, w), lambda i: (i, 0))
    wsp = pl.BlockSpec((SEG, d), lambda i: (0, 0))
    return pl.pallas_call(
        _merge_body,
        grid=(t // tm,),
        in_specs=[row(d), row(SEG), row(SEG), wsp, wsp],
        out_specs=row(d),
        out_shape=jax.ShapeDtypeStruct((t, d), F32),
        compiler_params=_params("parallel"),
        name="out_proj",
    )(x, oa, ob, w_a, w_b)


def kernel(x_prompt, x_sample, cache_a_k, cache_a_v, cache_b_k, cache_b_v, cache_b_logf, meta_tokens, g_ffn1, ffn1_w1, ffn1_w3, ffn1_w2, g_mix, w_in, b_f, g_qa, g_ka, g_qb, g_kb, lambda_q1, lambda_k1, lambda_q2, lambda_k2, g_oa, g_ob, w_out, g_ffn2, ffn2_w1, ffn2_w3, ffn2_w2, g_final):
    depth = w_in.shape[0]
    assert depth == 1, "meta rows skip attention, which is only valid for a single layer"
    bsz, seq, d = x_prompt.shape
    dbsz, dseq, _ = x_sample.shape
    past = cache_a_k.shape[2]
    n_small = dbsz * dseq
    lam_init = 0.8 - 0.6 * math.exp(-0.3 * 0)
    bf = lambda a: a.astype(BF16)

    w_main = bf(w_in[0, :, :N_SEG * SEG])
    w_f = bf(jnp.pad(w_in[0, :, N_SEG * SEG:], ((0, 0), (0, LANES - B_HEADS))))
    b_fp = jnp.pad(b_f[0], (0, LANES - B_HEADS)).reshape(1, LANES)
    lamv = jnp.stack([lambda_q1[0], lambda_k1[0], lambda_q2[0], lambda_k2[0]]).astype(F32)
    wo_a, wo_b = bf(w_out[0, :SEG]), bf(w_out[0, SEG:])
    f1 = (g_ffn1[0], bf(ffn1_w1[0]), bf(ffn1_w3[0]), bf(ffn1_w2[0]), g_mix[0])
    f2 = (g_ffn2[0], bf(ffn2_w1[0]), bf(ffn2_w3[0]), bf(ffn2_w2[0]), g_final[0])

    xp = x_prompt.reshape(bsz * seq, d)
    n_pad = -(n_small + N_META) % SMALL_ROW_MULT
    xs = jnp.concatenate([x_sample.reshape(n_small, d), meta_tokens.astype(x_sample.dtype),
                          jnp.zeros((n_pad, d), x_sample.dtype)], axis=0)
    pos_p = N_META + jnp.arange(seq)
    pos_s = jnp.concatenate([jnp.tile(past + jnp.arange(dseq), dbsz), jnp.arange(N_META),
                             jnp.zeros((n_pad,), jnp.int32)])

    x1p, hp = _ffn(xp, *f1, mode="emit_norm")
    x1s, hs = _ffn(xs, *f1, mode="emit_norm")
    proj = functools.partial(_project, w_main=w_main, w_f=w_f, b_f=b_fp, g_qa=g_qa[0], g_ka=g_ka[0],
                             g_qb=g_qb[0], g_kb=g_kb[0])
    diff_segs, fox_segs = (0, 1, 2), (3, 4, 5)
    qa_p, kaf_p, kab_p, vaf_p, vab_p, logf_p = proj(hp, pos=pos_p, segs=diff_segs, gate=True, rows_per_seq=seq, gap=N_META)
    qb_p, kbf_p, kbb_p, vbf_p, vbb_p = proj(hp, pos=pos_p, segs=fox_segs, gate=False, rows_per_seq=seq, gap=N_META)
    qa_s, kaf_s, kab_s, vaf_s, vab_s, logf_s = proj(hs, pos=pos_s, segs=diff_segs, gate=True, rows_per_seq=xs.shape[0])
    qb_s, kbf_s, kbb_s, vbf_s, vbb_s = proj(hs, pos=pos_s, segs=fox_segs, gate=False, rows_per_seq=xs.shape[0])

    logf_meta = logf_s[n_small:n_small + N_META]
    c_real = _cumsum_lanes(logf_p.reshape(bsz, seq, B_HEADS).transpose(0, 2, 1).reshape(bsz * B_HEADS, seq))
    c_meta = _cumsum_lanes(logf_meta.T)
    seq_s = jnp.concatenate([cache_b_logf[0].astype(F32), logf_s[:n_small].reshape(dbsz, dseq, B_HEADS)], axis=1)
    c_small = _cumsum_lanes(seq_s.transpose(0, 2, 1).reshape(dbsz * B_HEADS, past + dseq))
    tq = _attn_tile(seq)
    negc_p = (-c_real).reshape(bsz, PROMPT_GROUPS, PROMPT_UNITS, seq // tq, tq)
    negc_meta = (c_meta[:, N_META - 1:] - c_meta).reshape(PROMPT_GROUPS, PROMPT_UNITS, N_META)
    tb = _pick_tile(past, SAMPLE_BLOCK, LANES)
    negc_s_old = (-c_small[:, :past]).reshape(dbsz, B_HEADS, past // tb, tb)
    negc_s_new = (-c_small[:, past:]).reshape(dbsz, B_HEADS, dseq)

    meta = slice(n_small, n_small + N_META)
    oa_p = _attn_prompt("a", qa_p, kab_p, vab_p, kab_s[meta], vab_s[meta], (lamv,), g_oa[0],
                        batch=bsz, seq=seq, lam_init=lam_init)
    ob_p = _attn_prompt("b", qb_p, kbb_p, vbb_p, kbb_s[meta], vbb_s[meta], (negc_p, negc_meta), g_ob[0],
                        batch=bsz, seq=seq)
    rows = lambda c: c[0].reshape(-1, HEAD_DIM)
    cva = cache_a_v[0].reshape(dbsz, past, A_HEADS, 2, HEAD_DIM).swapaxes(2, 3).reshape(-1, HEAD_DIM)
    oa_s, ob_s = _attn_sample(
        qa_s, kab_s, vab_s, rows(cache_a_k), cva, qb_s, kbb_s, vbb_s, rows(cache_b_k), rows(cache_b_v),
        negc_s_old, negc_s_new, lamv, g_oa[0], g_ob[0], batch=dbsz, t=dseq, past=past, lam_init=lam_init)

    y_p = _ffn(_merge(x1p, oa_p, ob_p, wo_a, wo_b), *f2, mode="final_norm")
    y_s = _ffn(_merge(x1s[:n_small], oa_s, ob_s, wo_a, wo_b), *f2, mode="final_norm")

    def unslot(x, lead, tail, slots=None):
        if slots == VA_SLOTS:
            x = x.reshape(lead + (2, A_HEADS, HEAD_DIM)).swapaxes(-3, -2)
        return x.reshape(lead + tail)

    small_f32 = (kaf_s, vaf_s, kbf_s, vbf_s)
    meta_rows = [x[n_small * ROW_SLOTS:(n_small + N_META) * ROW_SLOTS] for x in small_f32]
    full = _fill_gaps((kaf_p, vaf_p, kbf_p, vbf_p), meta_rows, n_seq=bsz)
    tails = ((A_HEADS, 2, HEAD_DIM), (A_HEADS, 2 * HEAD_DIM), (B_HEADS, HEAD_DIM), (B_HEADS, HEAD_DIM))
    slots = (None, VA_SLOTS, None, None)
    ak_p, av_p, bk_p, bv_p = [unslot(x, (1, bsz, N_META + seq), tl, sl) for x, tl, sl in zip(full, tails, slots)]
    ak_s, av_s, bk_s, bv_s = [unslot(x[:n_small * ROW_SLOTS], (1, dbsz, dseq), tl, sl)
                              for x, tl, sl in zip(small_f32, tails, slots)]
    lf_meta = jnp.broadcast_to(logf_meta[None], (bsz, N_META, B_HEADS))
    lf_p = jnp.concatenate([lf_meta, logf_p.reshape(bsz, seq, B_HEADS)], axis=1)[None]
    lf_s = logf_s[:n_small].reshape(1, dbsz, dseq, B_HEADS)
    return (y_p.reshape(bsz, seq, d), y_s.reshape(dbsz, dseq, d),
            ak_p, av_p, bk_p, bv_p, lf_p, ak_s, av_s, bk_s, bv_s, lf_s)
```
